```python
import math
import jax, jax.numpy as jnp
from jax import lax
import numpy as np

D_MODEL = 2048
BATCH = 1
SEQ = 16384
DEPTH = 2

ATT_HEADS = 8
ATT_QK_DIM = 64
ATT_V_DIM = 2 * ATT_QK_DIM
ATT_WIDTH = ATT_HEADS * ATT_V_DIM
Q_BLOCK = 128
ROPE_THETA = 500000.0
ROT_DIM = ATT_QK_DIM // 4
POOL_WINDOWS = (2, 4, 8, 16)
N_POOL = len(POOL_WINDOWS)
POOL_CH = 128
POOL_WIDTH = N_POOL * POOL_CH
SSM_GROUP_CH = 16
SSM_WIDTH = 512
SSM_GROUPS = SSM_WIDTH // SSM_GROUP_CH
SSM_STATE = 64
D_FF = 4 * D_MODEL
N_BRANCH = 3
DEEPNORM_ALPHA = (2.0 * DEPTH) ** 0.25
DEEPNORM_BETA = (8.0 * DEPTH) ** -0.25
LN_EPS = 1e-5

Q_COLS = ATT_HEADS * 2 * ATT_QK_DIM
K_COLS = ATT_HEADS * 2 * ATT_QK_DIM
V_COLS = ATT_HEADS * ATT_V_DIM
GATE_COLS = N_BRANCH * D_MODEL
IN_COLS = Q_COLS + K_COLS + V_COLS + POOL_WIDTH + SSM_WIDTH + GATE_COLS
SPLITS = [Q_COLS, Q_COLS + K_COLS, Q_COLS + K_COLS + V_COLS,
          Q_COLS + K_COLS + V_COLS + POOL_WIDTH,
          Q_COLS + K_COLS + V_COLS + POOL_WIDTH + SSM_WIDTH]

kernel_name = "hybrid_diffattn_pool_s5_deepnorm"


def layer_norm(x, g, b):
    xf = x.astype(jnp.float32)
    mu = jnp.mean(xf, axis=-1, keepdims=True)
    var = jnp.mean(jnp.square(xf - mu), axis=-1, keepdims=True)
    return ((xf - mu) * lax.rsqrt(var + LN_EPS)).astype(x.dtype) * g + b


def rotary_tables(positions):
    inv_freq = ROPE_THETA ** (-jnp.arange(0, ROT_DIM, 2, dtype=jnp.float32) / ROT_DIM)
    ang = positions.astype(jnp.float32)[..., None] * inv_freq
    return jnp.cos(ang)[:, :, None, None, :], jnp.sin(ang)[:, :, None, None, :]


def apply_partial_rotary(t, cos, sin):
    half = ROT_DIM // 2
    tr = t[..., :ROT_DIM].astype(jnp.float32)
    t1, t2 = tr[..., :half], tr[..., half:]
    rot = jnp.concatenate([t1 * cos - t2 * sin, t2 * cos + t1 * sin], axis=-1).astype(t.dtype)
    return jnp.concatenate([rot, t[..., ROT_DIM:]], axis=-1)


def diff_attention(q, k, v, lam):
    bsz, s = q.shape[0], q.shape[1]
    nb = s // Q_BLOCK
    scale = 1.0 / math.sqrt(ATT_QK_DIM)
    k1, k2 = k[..., 0, :], k[..., 1, :]
    qb1 = jnp.moveaxis(q[..., 0, :].reshape(bsz, nb, Q_BLOCK, ATT_HEADS, ATT_QK_DIM), 1, 0)
    qb2 = jnp.moveaxis(q[..., 1, :].reshape(bsz, nb, Q_BLOCK, ATT_HEADS, ATT_QK_DIM), 1, 0)
    kpos = jnp.arange(s)
    neg = jnp.finfo(jnp.float32).min

    def block(args):
        i, a1, a2 = args
        qpos = i * Q_BLOCK + jnp.arange(Q_BLOCK)
        mask = (kpos[None, :] <= qpos[:, None])[None, None]
        s1 = jnp.einsum('bqhd,bkhd->bhqk', a1, k1).astype(jnp.float32) * scale
        s2 = jnp.einsum('bqhd,bkhd->bhqk', a2, k2).astype(jnp.float32) * scale
        p1 = jax.nn.softmax(jnp.where(mask, s1, neg), axis=-1)
        p2 = jax.nn.softmax(jnp.where(mask, s2, neg), axis=-1)
        attn = (p1 - lam * p2).astype(v.dtype)
        return jnp.einsum('bhqk,bkhd->bqhd', attn, v)

    out = lax.map(block, (jnp.arange(nb), qb1, qb2))
    return jnp.moveaxis(out, 0, 1).reshape(bsz, s, ATT_HEADS, ATT_V_DIM)


def multiscale_pool(u, w_pool, pool_scale):
    bsz, s, _ = u.shape
    uf = u.astype(jnp.float32).reshape(bsz, s, N_POOL, POOL_CH)
    cs0 = jnp.concatenate([jnp.zeros((bsz, 1, N_POOL, POOL_CH), jnp.float32),
                           jnp.cumsum(uf, axis=1)], axis=1)
    t = jnp.arange(s)
    outs = []
    for g, w in enumerate(POOL_WINDOWS):
        lo = jnp.pad(cs0[:, :, g], ((0, 0), (w - 1, 0), (0, 0)))[:, :s]
        cnt = jnp.minimum(t + 1, w).astype(jnp.float32)[None, :, None]
        outs.append((cs0[:, 1:, g] - lo) / cnt - uf[:, :, g])
    pooled = jnp.stack(outs, axis=2).astype(u.dtype)
    mixed = jnp.einsum('bsgc,gcd->bsgd', pooled, w_pool)
    return mixed.reshape(bsz, s, POOL_WIDTH) * pool_scale


def s5_ssm(u, a_re, a_im, log_dt, b_re, b_im, c_re, c_im, d_skip):
    f32 = jnp.float32
    bsz, s, _ = u.shape
    uf = u.astype(f32).reshape(bsz, s, SSM_GROUPS, SSM_GROUP_CH)
    ar, ai = a_re.astype(f32), a_im.astype(f32)
    dt = jnp.exp(log_dt.astype(f32))[:, None]
    mag = jnp.exp(ar * dt)
    ab_re, ab_im = mag * jnp.cos(ai * dt), mag * jnp.sin(ai * dt)
    den = ar * ar + ai * ai
    nr, ni = ab_re - 1.0, ab_im
    f_re = (nr * ar + ni * ai) / den
    f_im = (ni * ar - nr * ai) / den
    br, bi = b_re.astype(f32), b_im.astype(f32)
    bb_re = f_re[..., None] * br - f_im[..., None] * bi
    bb_im = f_re[..., None] * bi + f_im[..., None] * br
    bu_re = jnp.einsum('bsgh,gph->bsgp', uf, bb_re)
    bu_im = jnp.einsum('bsgh,gph->bsgp', uf, bb_im)
    a_re_t = jnp.broadcast_to(ab_re, bu_re.shape)
    a_im_t = jnp.broadcast_to(ab_im, bu_im.shape)

    def combine(e1, e2):
        a1r, a1i, b1r, b1i = e1
        a2r, a2i, b2r, b2i = e2
        return (a2r * a1r - a2i * a1i, a2r * a1i + a2i * a1r,
                a2r * b1r - a2i * b1i + b2r, a2r * b1i + a2i * b1r + b2i)

    _, _, xr, xi = lax.associative_scan(combine, (a_re_t, a_im_t, bu_re, bu_im), axis=1)
    y = (jnp.einsum('bsgp,ghp->bsgh', xr, c_re.astype(f32))
         - jnp.einsum('bsgp,ghp->bsgh', xi, c_im.astype(f32))
         + d_skip.astype(f32).reshape(SSM_GROUPS, SSM_GROUP_CH) * uf)
    return y.reshape(bsz, s, SSM_WIDTH).astype(u.dtype)


def setup_inputs(seed: int = 0) -> dict:
    key = jax.random.key(seed)
    ks = jax.random.split(key, 32)
    L = DEPTH

    def nrm(k, shape, scale):
        return jax.random.normal(k, shape, jnp.float32) * scale

    n_idx = jnp.arange(SSM_STATE, dtype=jnp.float32)
    return {
        "x": nrm(ks[0], (BATCH, SEQ, D_MODEL), 1.0),
        "positions": jnp.broadcast_to(jnp.arange(SEQ, dtype=jnp.int32)[None, :], (BATCH, SEQ)),
        "ln_in_g": 1.0 + nrm(ks[1], (D_MODEL,), 0.02),
        "ln_in_b": nrm(ks[2], (D_MODEL,), 0.02),
        "w_in": nrm(ks[3], (L, D_MODEL, IN_COLS), D_MODEL ** -0.5),
        "b_gate": nrm(ks[4], (L, GATE_COLS), 0.02),
        "lam_q1": nrm(ks[5], (L, ATT_QK_DIM), 0.1),
        "lam_k1": nrm(ks[6], (L, ATT_QK_DIM), 0.1),
        "lam_q2": nrm(ks[7], (L, ATT_QK_DIM), 0.1),
        "lam_k2": nrm(ks[8], (L, ATT_QK_DIM), 0.1),
        "subln_g": 1.0 + nrm(ks[9], (L, ATT_V_DIM), 0.02),
        "pool_w": nrm(ks[10], (L, N_POOL, POOL_CH, POOL_CH), POOL_CH ** -0.5),
        "pool_scale": 1.0 + nrm(ks[11], (L, POOL_WIDTH), 0.02),
        "ssm_a_re": -0.5 * jnp.exp(nrm(ks[12], (L, SSM_GROUPS, SSM_STATE), 0.01)),
        "ssm_a_im": math.pi * n_idx + nrm(ks[13], (L, SSM_GROUPS, SSM_STATE), 0.01),
        "ssm_log_dt": jax.random.uniform(ks[14], (L, SSM_GROUPS), jnp.float32,
                                         math.log(1e-3), math.log(1e-1)),
        "ssm_b_re": nrm(ks[15], (L, SSM_GROUPS, SSM_STATE, SSM_GROUP_CH), (2.0 * SSM_GROUP_CH) ** -0.5),
        "ssm_b_im": nrm(ks[16], (L, SSM_GROUPS, SSM_STATE, SSM_GROUP_CH), (2.0 * SSM_GROUP_CH) ** -0.5),
        "ssm_c_re": nrm(ks[17], (L, SSM_GROUPS, SSM_GROUP_CH, SSM_STATE), 0.5),
        "ssm_c_im": nrm(ks[18], (L, SSM_GROUPS, SSM_GROUP_CH, SSM_STATE), 0.5),
        "ssm_d": nrm(ks[19], (L, SSM_WIDTH), 1.0),
        "glu_w": nrm(ks[20], (L, SSM_WIDTH, SSM_WIDTH), SSM_WIDTH ** -0.5),
        "glu_b": nrm(ks[21], (L, SSM_WIDTH), 0.02),
        "proj_attn": nrm(ks[22], (L, ATT_WIDTH, D_MODEL), ATT_WIDTH ** -0.5),
        "proj_pool": nrm(ks[23], (L, POOL_WIDTH, D_MODEL), POOL_WIDTH ** -0.5),
        "proj_ssm": nrm(ks[24], (L, SSM_WIDTH, D_MODEL), SSM_WIDTH ** -0.5),
        "w_out": nrm(ks[25], (L, D_MODEL, D_MODEL), DEEPNORM_BETA * D_MODEL ** -0.5),
        "ln1_g": 1.0 + nrm(ks[26], (L, D_MODEL), 0.02),
        "ln1_b": nrm(ks[27], (L, D_MODEL), 0.02),
        "w_up": nrm(ks[28], (L, D_MODEL, D_FF), D_MODEL ** -0.5),
        "w_down": nrm(ks[29], (L, D_FF, D_MODEL), DEEPNORM_BETA * D_FF ** -0.5),
        "ln2_g": 1.0 + nrm(ks[30], (L, D_MODEL), 0.02),
        "ln2_b": nrm(ks[31], (L, D_MODEL), 0.02),
    }


def reference(x, positions, ln_in_g, ln_in_b, w_in, b_gate, lam_q1, lam_k1, lam_q2, lam_k2,
              subln_g, pool_w, pool_scale, ssm_a_re, ssm_a_im, ssm_log_dt, ssm_b_re, ssm_b_im,
              ssm_c_re, ssm_c_im, ssm_d, glu_w, glu_b, proj_attn, proj_pool, proj_ssm, w_out,
              ln1_g, ln1_b, w_up, w_down, ln2_g, ln2_b):
    bsz, s, _ = x.shape
    h = layer_norm(x, ln_in_g, ln_in_b)
    cos, sin = rotary_tables(positions)
    for l in range(DEPTH):
        lam_init = 0.8 - 0.6 * math.exp(-0.3 * l)
        proj = h @ w_in[l]
        q, k, v, u_pool, u_ssm, g = jnp.split(proj, SPLITS, axis=-1)
        gates = jax.nn.sigmoid(g + b_gate[l]).reshape(bsz, s, N_BRANCH, D_MODEL)

        q = apply_partial_rotary(q.reshape(bsz, s, ATT_HEADS, 2, ATT_QK_DIM), cos, sin)
        k = apply_partial_rotary(k.reshape(bsz, s, ATT_HEADS, 2, ATT_QK_DIM), cos, sin)
        v = v.reshape(bsz, s, ATT_HEADS, ATT_V_DIM)
        f32 = jnp.float32
        lam = (jnp.exp(jnp.sum(lam_q1[l].astype(f32) * lam_k1[l].astype(f32)))
               - jnp.exp(jnp.sum(lam_q2[l].astype(f32) * lam_k2[l].astype(f32))) + lam_init)
        o = diff_attention(q, k, v, lam)
        of = o.astype(f32)
        o = (of * lax.rsqrt(jnp.mean(of * of, axis=-1, keepdims=True) + LN_EPS)).astype(x.dtype)
        y_attn = (o * subln_g[l] * (1.0 - lam_init)).reshape(bsz, s, ATT_WIDTH)

        y_pool = multiscale_pool(u_pool, pool_w[l], pool_scale[l])

        y_s = jax.nn.gelu(s5_ssm(u_ssm, ssm_a_re[l], ssm_a_im[l], ssm_log_dt[l], ssm_b_re[l],
                                 ssm_b_im[l], ssm_c_re[l], ssm_c_im[l], ssm_d[l]))
        y_ssm = y_s * jax.nn.sigmoid(y_s @ glu_w[l] + glu_b[l])

        merged = (gates[:, :, 0] * (y_attn @ proj_attn[l])
                  + gates[:, :, 1] * (y_pool @ proj_pool[l])
                  + gates[:, :, 2] * (y_ssm @ proj_ssm[l]))
        h = layer_norm(DEEPNORM_ALPHA * h + merged @ w_out[l], ln1_g[l], ln1_b[l])

        ff = jnp.square(jax.nn.relu(h @ w_up[l])) @ w_down[l]
        h = layer_norm(DEEPNORM_ALPHA * h + ff, ln2_g[l], ln2_b[l])
    return h
```

```python
import functools
import math

import jax
import jax.numpy as jnp
from jax import lax
from jax.experimental import pallas as pl
from jax.experimental.pallas import tpu as pltpu

F32 = jnp.float32
BF16 = jnp.bfloat16

D_MODEL = 2048
ATT_HEADS = 8
ATT_QK_DIM = 64
HEAD_W = 2 * ATT_QK_DIM
ATT_WIDTH = ATT_HEADS * HEAD_W
ROPE_THETA = 500000.0
ROT_DIM = ATT_QK_DIM // 4
ROT_HALF = ROT_DIM // 2
POOL_WINDOWS = (2, 4, 8, 16)
POOL_CH = 128
POOL_WIDTH = len(POOL_WINDOWS) * POOL_CH
POOL_HALO = 16
SSM_GROUP_CH = 16
SSM_WIDTH = 512
SSM_GROUPS = SSM_WIDTH // SSM_GROUP_CH
SSM_STATE = 64
D_FF = 4 * D_MODEL
LN_EPS = 1e-5
Q_COLS = ATT_WIDTH
V_OFF = 2 * ATT_WIDTH
POOL_OFF = 3 * ATT_WIDTH
SSM_OFF = POOL_OFF + POOL_WIDTH
GATE_OFF = SSM_OFF + SSM_WIDTH
GATE_COLS = 3 * D_MODEL

LANES = 128
SSM_SLABS = SSM_WIDTH // LANES
SLAB_GROUPS = LANES // SSM_GROUP_CH
SLAB_STATE = SLAB_GROUPS * SSM_STATE
SSM_CHUNK = 16
VMEM_LIMIT = 56 * 1024 * 1024

NEG_BIG = -1e30


def _tiles(seq):
    return dict(
        ln=min(512, seq),
        proj=min(1024, seq),
        att=min(512, seq),
        pool=min(1024, seq),
        ssm=min(256, seq // SSM_CHUNK),
        merge=min(256, seq),
        mlp=min(512, seq),
        ff=512,
    )


def _cparams(sem):
    return pltpu.CompilerParams(dimension_semantics=sem, vmem_limit_bytes=VMEM_LIMIT)


def _ln_rows(z, g, b):
    mu = jnp.mean(z, axis=-1, keepdims=True)
    zc = z - mu
    var = jnp.mean(zc * zc, axis=-1, keepdims=True)
    return zc * lax.rsqrt(var + LN_EPS) * g + b


def _ln_kernel(x_ref, g_ref, b_ref, h_ref, hb_ref):
    y = _ln_rows(x_ref[...], g_ref[...], b_ref[...])
    h_ref[...] = y
    hb_ref[...] = y.astype(BF16)


def _layer_norm(x2d, g, b, tm):
    s, d = x2d.shape
    row = pl.BlockSpec((tm, d), lambda i: (i, 0))
    vec = pl.BlockSpec((1, d), lambda i: (0, 0))
    return pl.pallas_call(
        _ln_kernel,
        grid=(s // tm,),
        in_specs=[row, vec, vec],
        out_specs=[row, row],
        out_shape=[jax.ShapeDtypeStruct((s, d), F32), jax.ShapeDtypeStruct((s, d), BF16)],
        compiler_params=_cparams(("parallel",)),
        name="ln_in",
    )(x2d, g.reshape(1, d), b.reshape(1, d))


def _proj_qk_kernel(x_ref, w_ref, pos_ref, invf_ref, o_ref, *, tn):
    j = pl.program_id(1)
    acc = jnp.dot(x_ref[...], w_ref[...], preferred_element_type=F32)
    ang = pos_ref[...].astype(F32) * invf_ref[...]
    cos, sin = jnp.cos(ang), jnp.sin(ang)
    d = lax.broadcasted_iota(jnp.int32, ang.shape, 1) % ATT_QK_DIM
    scale = jnp.where(j * tn < Q_COLS, 1.0 / math.sqrt(ATT_QK_DIM), 1.0).astype(F32)
    c_same = jnp.where(d < ROT_DIM, cos, 1.0) * scale
    c_up = jnp.where(d < ROT_HALF, -sin, 0.0) * scale
    c_dn = jnp.where((d >= ROT_HALF) & (d < ROT_DIM), sin, 0.0) * scale
    for c in range(tn // LANES):
        xg = acc[:, c * LANES:(c + 1) * LANES]
        up = pltpu.roll(xg, LANES - ROT_HALF, axis=1)
        dn = pltpu.roll(xg, ROT_HALF, axis=1)
        o_ref[:, c * LANES:(c + 1) * LANES] = (xg * c_same + up * c_up + dn * c_dn).astype(o_ref.dtype)


def _proj_plain_kernel(x_ref, w_ref, o_ref):
    o_ref[...] = jnp.dot(x_ref[...], w_ref[...], preferred_element_type=F32).astype(o_ref.dtype)


def _proj_slab_kernel(x_ref, w_ref, o_ref):
    acc = jnp.dot(x_ref[...], w_ref[...], preferred_element_type=F32)
    for c in range(o_ref.shape[0]):
        o_ref[c] = acc[:, c * LANES:(c + 1) * LANES]


def _proj_gate_kernel(x_ref, w_ref, b_ref, o_ref):
    acc = jnp.dot(x_ref[...], w_ref[...], preferred_element_type=F32)
    o_ref[...] = jax.nn.sigmoid(acc + b_ref[...])


def _in_projections(hb, w_in_b, b_gate, pos_col, invf, tm):
    s, d = hb.shape
    tn = 512
    x_spec = pl.BlockSpec((tm, d), lambda i, j: (i, 0))

    def w_spec(off):
        return pl.BlockSpec((d, tn), lambda i, j: (0, j + off // tn))

    out_tile = pl.BlockSpec((tm, tn), lambda i, j: (i, j))
    sem = _cparams(("parallel", "arbitrary"))

    qk = pl.pallas_call(
        functools.partial(_proj_qk_kernel, tn=tn),
        grid=(s // tm, V_OFF // tn),
        in_specs=[x_spec, w_spec(0),
                  pl.BlockSpec((tm, 1), lambda i, j: (i, 0)),
                  pl.BlockSpec((1, LANES), lambda i, j: (0, 0))],
        out_specs=out_tile,
        out_shape=jax.ShapeDtypeStruct((s, V_OFF), BF16),
        compiler_params=sem, name="proj_qk",
    )(hb, w_in_b, pos_col, invf)

    v = pl.pallas_call(
        _proj_plain_kernel,
        grid=(s // tm, ATT_WIDTH // tn),
        in_specs=[x_spec, w_spec(V_OFF)],
        out_specs=out_tile,
        out_shape=jax.ShapeDtypeStruct((s, ATT_WIDTH), BF16),
        compiler_params=sem, name="proj_v",
    )(hb, w_in_b)

    u_pool = pl.pallas_call(
        _proj_plain_kernel,
        grid=(s // tm, POOL_WIDTH // tn),
        in_specs=[x_spec, w_spec(POOL_OFF)],
        out_specs=out_tile,
        out_shape=jax.ShapeDtypeStruct((s, POOL_WIDTH), F32),
        compiler_params=sem, name="proj_pool_in",
    )(hb, w_in_b)

    u_ssm = pl.pallas_call(
        _proj_slab_kernel,
        grid=(s // tm, SSM_WIDTH // tn),
        in_specs=[x_spec, w_spec(SSM_OFF)],
        out_specs=pl.BlockSpec((SSM_SLABS, tm, LANES), lambda i, j: (0, i, 0)),
        out_shape=jax.ShapeDtypeStruct((SSM_SLABS, s, LANES), F32),
        compiler_params=sem, name="proj_ssm_in",
    )(hb, w_in_b)

    gates = pl.pallas_call(
        _proj_gate_kernel,
        grid=(s // tm, GATE_COLS // tn),
        in_specs=[x_spec, w_spec(GATE_OFF), pl.BlockSpec((1, tn), lambda i, j: (0, j))],
        out_specs=out_tile,
        out_shape=jax.ShapeDtypeStruct((s, GATE_COLS), F32),
        compiler_params=sem, name="proj_gates",
    )(hb, w_in_b, b_gate.reshape(1, GATE_COLS))
    return qk, v, u_pool, u_ssm, gates


def _attn_kernel(lam_ref, qt_ref, k_ref, vt_ref, g_ref, o_ref,
                 m1_ref, l1_ref, a1_ref, m2_ref, l2_ref, a2_ref, *, tq):
    i = pl.program_id(1)
    qt = qt_ref[0]
    comp = lax.broadcasted_iota(jnp.int32, qt.shape, 0) < ATT_QK_DIM
    zero = jnp.zeros_like(qt)
    q1t = jnp.where(comp, qt, zero)
    q2t = jnp.where(comp, zero, qt)

    for m_ref, l_ref, a_ref in ((m1_ref, l1_ref, a1_ref), (m2_ref, l2_ref, a2_ref)):
        m_ref[...] = jnp.full(m_ref.shape, NEG_BIG, F32)
        l_ref[...] = jnp.zeros(l_ref.shape, F32)
        a_ref[...] = jnp.zeros(a_ref.shape, F32)

    def update(s, vb, m_ref, l_ref, a_ref):
        m_old = m_ref[...]
        m_new = jnp.maximum(m_old, jnp.max(s, axis=0, keepdims=True))
        alpha = jnp.exp(m_old - m_new)
        p = jnp.exp(s - m_new)
        l_ref[...] = alpha * l_ref[...] + jnp.sum(p, axis=0, keepdims=True)
        a_ref[...] = alpha * a_ref[...] + jnp.dot(vb, p.astype(BF16), preferred_element_type=F32)
        m_ref[...] = m_new

    def block(j, masked):
        kb = k_ref[pl.ds(pl.multiple_of(j * tq, tq), tq), :]
        vb = vt_ref[j]
        s1 = jnp.dot(kb, q1t, preferred_element_type=F32)
        s2 = jnp.dot(kb, q2t, preferred_element_type=F32)
        if masked:
            keep = (lax.broadcasted_iota(jnp.int32, s1.shape, 0)
                    <= lax.broadcasted_iota(jnp.int32, s1.shape, 1))
            s1 = jnp.where(keep, s1, NEG_BIG)
            s2 = jnp.where(keep, s2, NEG_BIG)
        update(s1, vb, m1_ref, l1_ref, a1_ref)
        update(s2, vb, m2_ref, l2_ref, a2_ref)

    def full_block(j, carry):
        block(j, False)
        return carry

    lax.fori_loop(0, i, full_block, 0)
    block(i, True)

    lam = lam_ref[0, 0]
    o = a1_ref[...] / l1_ref[...] - lam * (a2_ref[...] / l2_ref[...])
    o = o * lax.rsqrt(jnp.mean(o * o, axis=0, keepdims=True) + LN_EPS) * g_ref[...]
    o_ref[...] = o.T.astype(o_ref.dtype)


def _attention(qk, v, lam, g_col, tq):
    s = qk.shape[0]
    nb = s // tq
    qt = qk[:, :Q_COLS].reshape(nb, tq, ATT_WIDTH).transpose(0, 2, 1)
    vt = v.reshape(nb, tq, ATT_WIDTH).transpose(0, 2, 1)
    stat = pltpu.VMEM((1, tq), F32)
    accum = pltpu.VMEM((HEAD_W, tq), F32)
    return pl.pallas_call(
        functools.partial(_attn_kernel, tq=tq),
        grid=(ATT_HEADS, nb),
        in_specs=[
            pl.BlockSpec(memory_space=pltpu.SMEM),
            pl.BlockSpec((1, HEAD_W, tq), lambda h, i: (i, h, 0)),
            pl.BlockSpec((s, HEAD_W), lambda h, i: (0, ATT_HEADS + h)),
            pl.BlockSpec((nb, HEAD_W, tq), lambda h, i: (0, h, 0)),
            pl.BlockSpec((HEAD_W, 1), lambda h, i: (0, 0)),
        ],
        out_specs=pl.BlockSpec((tq, HEAD_W), lambda h, i: (i, h)),
        out_shape=jax.ShapeDtypeStruct((s, ATT_WIDTH), BF16),
        scratch_shapes=[stat, stat, accum, stat, stat, accum],
        compiler_params=_cparams(("parallel", "arbitrary")),
        name="diff_attention",
    )(lam, qt, qk, vt, g_col)


def _pool_kernel(u_ref, halo_ref, w_ref, sc_ref, o_ref, buf_ref, *, tm):
    i = pl.program_id(0)
    halo = halo_ref[...]
    buf_ref[0:POOL_HALO, :] = jnp.where(i > 0, halo, jnp.zeros_like(halo))
    buf_ref[POOL_HALO:, :] = u_ref[...]
    t = i * tm + lax.broadcasted_iota(jnp.int32, (tm, 1), 0)
    for g, w in enumerate(POOL_WINDOWS):
        cols = slice(g * POOL_CH, (g + 1) * POOL_CH)
        acc = buf_ref[POOL_HALO:, cols]
        for back in range(1, w):
            acc = acc + buf_ref[POOL_HALO - back:POOL_HALO - back + tm, cols]
        cnt = jnp.minimum(t + 1, w).astype(F32)
        pooled = acc / cnt - buf_ref[POOL_HALO:, cols]
        mixed = jnp.dot(pooled.astype(BF16), w_ref[g], preferred_element_type=F32)
        o_ref[:, cols] = (mixed * sc_ref[:, cols]).astype(o_ref.dtype)


def _pool(u_pool, pool_w_b, pool_scale, tm):
    s = u_pool.shape[0]
    return pl.pallas_call(
        functools.partial(_pool_kernel, tm=tm),
        grid=(s // tm,),
        in_specs=[
            pl.BlockSpec((tm, POOL_WIDTH), lambda i: (i, 0)),
            pl.BlockSpec((POOL_HALO, POOL_WIDTH),
                         lambda i: (jnp.maximum(i * (tm // POOL_HALO) - 1, 0), 0)),
            pl.BlockSpec((len(POOL_WINDOWS), POOL_CH, POOL_CH), lambda i: (0, 0, 0)),
            pl.BlockSpec((1, POOL_WIDTH), lambda i: (0, 0)),
        ],
        out_specs=pl.BlockSpec((tm, POOL_WIDTH), lambda i: (i, 0)),
        out_shape=jax.ShapeDtypeStruct((s, POOL_WIDTH), BF16),
        scratch_shapes=[pltpu.VMEM((tm + POOL_HALO, POOL_WIDTH), F32)],
        compiler_params=_cparams(("parallel",)),
        name="pool",
    )(u_pool, u_pool, pool_w_b, pool_scale.reshape(1, POOL_WIDTH))


def _ssm_matrices(a_re, a_im, log_dt, b_re, b_im, c_re, c_im, d_skip):
    hp = lax.Precision.HIGHEST
    t_len = SSM_CHUNK
    ar, ai = a_re.astype(F32), a_im.astype(F32)
    dt = jnp.exp(log_dt.astype(F32))[:, None]
    mag = jnp.exp(ar * dt)
    ab_re, ab_im = mag * jnp.cos(ai * dt), mag * jnp.sin(ai * dt)
    den = ar * ar + ai * ai
    nr, ni = ab_re - 1.0, ab_im
    f_re = (nr * ar + ni * ai) / den
    f_im = (ni * ar - nr * ai) / den
    br, bi = b_re.astype(F32), b_im.astype(F32)
    bb_re = f_re[..., None] * br - f_im[..., None] * bi
    bb_im = f_re[..., None] * bi + f_im[..., None] * br
    kk = jnp.arange(t_len + 1, dtype=F32)[:, None, None]
    pmag = jnp.exp(ar * dt * kk)
    pw_re, pw_im = pmag * jnp.cos(ai * dt * kk), pmag * jnp.sin(ai * dt * kk)
    wb_re = pw_re[..., None] * bb_re - pw_im[..., None] * bb_im
    wb_im = pw_re[..., None] * bb_im + pw_im[..., None] * bb_re
    cr, ci = c_re.astype(F32), c_im.astype(F32)
    kmat = (jnp.einsum("kgpi,gop->kgio", wb_re, cr, precision=hp)
            - jnp.einsum("kgpi,gop->kgio", wb_im, ci, precision=hp))
    eye = jnp.eye(SLAB_GROUPS, dtype=F32)
    h = SSM_GROUP_CH

    lag = jnp.arange(t_len)[None, :] - jnp.arange(t_len)[:, None]
    kst = kmat[jnp.clip(lag, 0, t_len)] * (lag >= 0)[:, :, None, None, None].astype(F32)
    kst = kst.reshape(t_len, t_len, SSM_SLABS, SLAB_GROUPS, h, h)
    m_mat = jnp.einsum("stjgio,gm->jsgitmo", kst, eye)
    m_mat = m_mat.reshape(SSM_SLABS, t_len * LANES, t_len * LANES)

    rev = t_len - 1 - jnp.arange(t_len)
    pst = jnp.stack([wb_re[rev], wb_im[rev]], axis=0)
    pst = pst.reshape(2, t_len, SSM_SLABS, SLAB_GROUPS, SSM_STATE, h)
    p_mat = jnp.einsum("rsjgpi,gm->jsgirmp", pst, eye)
    p_mat = p_mat.reshape(SSM_SLABS, t_len * LANES, 2 * SLAB_STATE)

    pr, pi = pw_re[1:], pw_im[1:]
    q_re = pr[:, :, None, :] * cr[None] - pi[:, :, None, :] * ci[None]
    q_im = -pi[:, :, None, :] * cr[None] - pr[:, :, None, :] * ci[None]
    qst = jnp.stack([q_re, q_im], axis=0).reshape(2, t_len, SSM_SLABS, SLAB_GROUPS, h, SSM_STATE)
    q_mat = jnp.einsum("rtjgop,gm->jrgptmo", qst, eye)
    q_mat = q_mat.reshape(SSM_SLABS, 2 * SLAB_STATE, t_len * LANES)

    a_t = jnp.stack([pw_re[t_len], pw_im[t_len]], axis=0)
    a_t = a_t.reshape(2, SSM_SLABS, SLAB_STATE).transpose(1, 0, 2)
    d_row = jnp.tile(d_skip.astype(F32).reshape(SSM_SLABS, 1, LANES), (1, 1, t_len))
    return m_mat.astype(BF16), p_mat.astype(BF16), q_mat.astype(BF16), a_t, d_row


def _gelu_tanh(x):
    return x * (0.5 * (1.0 + jnp.tanh(math.sqrt(2.0 / math.pi) * (x + 0.044715 * (x * x * x)))))


def _ssm_kernel(u_ref, m_ref, p_ref, q_ref, at_ref, d_ref, o_ref, xloc_ref, xprev_ref, st_ref, *, cm):
    @pl.when(pl.program_id(1) == 0)
    def _():
        st_ref[...] = jnp.zeros(st_ref.shape, F32)

    u = u_ref[0]
    ub = u.astype(BF16)
    xloc_ref[...] = jnp.dot(ub, p_ref[0], preferred_element_type=F32)
    a_r = at_ref[0, 0:1, :]
    a_i = at_ref[0, 1:2, :]

    def step(c, carry):
        xr, xi = carry
        xprev_ref[pl.ds(c, 1), 0:SLAB_STATE] = xr
        xprev_ref[pl.ds(c, 1), SLAB_STATE:] = xi
        br = xloc_ref[pl.ds(c, 1), 0:SLAB_STATE]
        bi = xloc_ref[pl.ds(c, 1), SLAB_STATE:]
        return a_r * xr - a_i * xi + br, a_r * xi + a_i * xr + bi

    xr, xi = lax.fori_loop(0, cm, step, (st_ref[0:1, :], st_ref[1:2, :]))
    st_ref[0:1, :] = xr
    st_ref[1:2, :] = xi

    y = jnp.dot(ub, m_ref[0], preferred_element_type=F32)
    y = y + jnp.dot(xprev_ref[...].astype(BF16), q_ref[0], preferred_element_type=F32)
    y = y + d_ref[0] * u
    o_ref[0] = _gelu_tanh(y)


def _ssm(u_slabs, mats, cm):
    m_mat, p_mat, q_mat, a_t, d_row = mats
    nslab, s, _ = u_slabs.shape
    nchunk = s // SSM_CHUNK
    width = SSM_CHUNK * LANES
    u_rows = u_slabs.reshape(nslab, nchunk, width)
    rows = pl.BlockSpec((1, cm, width), lambda j, c: (j, c, 0))

    def per_slab(shape):
        return pl.BlockSpec((1,) + shape, lambda j, c: (j, 0, 0))

    y = pl.pallas_call(
        functools.partial(_ssm_kernel, cm=cm),
        grid=(nslab, nchunk // cm),
        in_specs=[rows, per_slab((width, width)), per_slab((width, 2 * SLAB_STATE)),
                  per_slab((2 * SLAB_STATE, width)), per_slab((2, SLAB_STATE)), per_slab((1, width))],
        out_specs=rows,
        out_shape=jax.ShapeDtypeStruct((nslab, nchunk, width), F32),
        scratch_shapes=[pltpu.VMEM((cm, 2 * SLAB_STATE), F32), pltpu.VMEM((cm, 2 * SLAB_STATE), F32),
                        pltpu.VMEM((2, SLAB_STATE), F32)],
        compiler_params=_cparams(("parallel", "arbitrary")),
        name="s5_ssm",
    )(u_rows, m_mat, p_mat, q_mat, a_t, d_row)
    return y.reshape(nslab, s, LANES)


def _merge_kernel(ya_ref, yp_ref, ys_ref, g_ref, h_ref, wa_ref, wp_ref, ws_ref, wg_ref, bg_ref,
                  wo_ref, lg_ref, lb_ref, o_ref, ob_ref, *, alpha):
    ys = jnp.concatenate([ys_ref[c] for c in range(SSM_SLABS)], axis=1)
    z = jnp.dot(ys.astype(BF16), wg_ref[...], preferred_element_type=F32) + bg_ref[...]
    y_ssm = ys * jax.nn.sigmoid(z)
    merged = g_ref[:, 0:D_MODEL] * jnp.dot(ya_ref[...], wa_ref[...], preferred_element_type=F32)
    merged = merged + g_ref[:, D_MODEL:2 * D_MODEL] * jnp.dot(
        yp_ref[...], wp_ref[...], preferred_element_type=F32)
    merged = merged + g_ref[:, 2 * D_MODEL:] * jnp.dot(
        y_ssm.astype(BF16), ws_ref[...], preferred_element_type=F32)
    z = alpha * h_ref[...] + jnp.dot(merged.astype(BF16), wo_ref[...], preferred_element_type=F32)
    y = _ln_rows(z, lg_ref[...], lb_ref[...])
    o_ref[...] = y
    ob_ref[...] = y.astype(BF16)


def _merge(y_attn, y_pool, y_s, gates, h, wa, wp, ws, wg, bg, wo, ln_g, ln_b, alpha, tm):
    s, d = h.shape

    def rows(width):
        return pl.BlockSpec((tm, width), lambda i: (i, 0))

    def whole(shape):
        return pl.BlockSpec(shape, lambda i: (0,) * len(shape), pipeline_mode=pl.Buffered(1))

    return pl.pallas_call(
        functools.partial(_merge_kernel, alpha=alpha),
        grid=(s // tm,),
        in_specs=[rows(ATT_WIDTH), rows(POOL_WIDTH),
                  pl.BlockSpec((SSM_SLABS, tm, LANES), lambda i: (0, i, 0)),
                  rows(GATE_COLS), rows(d),
                  whole((ATT_WIDTH, d)), whole((POOL_WIDTH, d)), whole((SSM_WIDTH, d)),
                  whole((SSM_WIDTH, SSM_WIDTH)), whole((1, SSM_WIDTH)), whole((d, d)),
                  whole((1, d)), whole((1, d))],
        out_specs=[rows(d), rows(d)],
        out_shape=[jax.ShapeDtypeStruct((s, d), F32), jax.ShapeDtypeStruct((s, d), BF16)],
        compiler_params=_cparams(("parallel",)),
        name="merge_out_ln",
    )(y_attn, y_pool, y_s, gates, h, wa, wp, ws, wg, bg.reshape(1, SSM_WIDTH), wo,
      ln_g.reshape(1, d), ln_b.reshape(1, d))


def _mlp_kernel(hb_ref, h_ref, wu_ref, wd_ref, lg_ref, lb_ref, o_ref, ob_ref, acc_ref, *, alpha):
    f = pl.program_id(1)

    @pl.when(f == 0)
    def _():
        acc_ref[...] = jnp.zeros(acc_ref.shape, F32)

    up = jnp.dot(hb_ref[...], wu_ref[...], preferred_element_type=F32)
    r = jnp.maximum(up, 0.0)
    acc_ref[...] += jnp.dot((r * r).astype(BF16), wd_ref[...], preferred_element_type=F32)

    @pl.when(f == pl.num_programs(1) - 1)
    def _():
        y = _ln_rows(alpha * h_ref[...] + acc_ref[...], lg_ref[...], lb_ref[...])
        o_ref[...] = y
        ob_ref[...] = y.astype(BF16)


def _mlp(hb, h, w_up_b, w_down_b, ln_g, ln_b, alpha, tm, tf):
    s, d = h.shape
    rows = pl.BlockSpec((tm, d), lambda i, f: (i, 0))
    vec = pl.BlockSpec((1, d), lambda i, f: (0, 0))
    return pl.pallas_call(
        functools.partial(_mlp_kernel, alpha=alpha),
        grid=(s // tm, D_FF // tf),
        in_specs=[rows, rows,
                  pl.BlockSpec((d, tf), lambda i, f: (0, f)),
                  pl.BlockSpec((tf, d), lambda i, f: (f, 0)),
                  vec, vec],
        out_specs=[rows, rows],
        out_shape=[jax.ShapeDtypeStruct((s, d), F32), jax.ShapeDtypeStruct((s, d), BF16)],
        scratch_shapes=[pltpu.VMEM((tm, d), F32)],
        compiler_params=_cparams(("parallel", "arbitrary")),
        name="mlp_ln",
    )(hb, h, w_up_b, w_down_b, ln_g.reshape(1, d), ln_b.reshape(1, d))


def kernel(x, positions, ln_in_g, ln_in_b, w_in, b_gate, lam_q1, lam_k1, lam_q2, lam_k2, subln_g, pool_w, pool_scale, ssm_a_re, ssm_a_im, ssm_log_dt, ssm_b_re, ssm_b_im, ssm_c_re, ssm_c_im, ssm_d, glu_w, glu_b, proj_attn, proj_pool, proj_ssm, w_out, ln1_g, ln1_b, w_up, w_down, ln2_g, ln2_b):
    bsz, seq, d = x.shape
    assert bsz == 1 and d == D_MODEL
    depth = w_in.shape[0]
    alpha = (2.0 * depth) ** 0.25
    tl = _tiles(seq)

    h, hb = _layer_norm(x.reshape(seq, d), ln_in_g, ln_in_b, tl["ln"])
    pos_col = positions.reshape(seq, 1)
    inv_freq = ROPE_THETA ** (-jnp.arange(0, ROT_DIM, 2, dtype=F32) / ROT_DIM)
    invf = jnp.tile(inv_freq, LANES // ROT_HALF).reshape(1, LANES)

    for l in range(depth):
        lam_init = 0.8 - 0.6 * math.exp(-0.3 * l)
        lam = (jnp.exp(jnp.sum(lam_q1[l].astype(F32) * lam_k1[l].astype(F32)))
               - jnp.exp(jnp.sum(lam_q2[l].astype(F32) * lam_k2[l].astype(F32))) + lam_init)
        g_col = (subln_g[l].astype(F32) * (1.0 - lam_init)).reshape(HEAD_W, 1)

        qk, v, u_pool, u_ssm, gates = _in_projections(
            hb, w_in[l].astype(BF16), b_gate[l], pos_col, invf, tl["proj"])
        y_attn = _attention(qk, v, lam.reshape(1, 1), g_col, tl["att"])
        y_pool = _pool(u_pool, pool_w[l].astype(BF16), pool_scale[l], tl["pool"])
        mats = _ssm_matrices(ssm_a_re[l], ssm_a_im[l], ssm_log_dt[l], ssm_b_re[l], ssm_b_im[l],
                             ssm_c_re[l], ssm_c_im[l], ssm_d[l])
        y_s = _ssm(u_ssm, mats, tl["ssm"])
        h, hb = _merge(y_attn, y_pool, y_s, gates, h,
                       proj_attn[l].astype(BF16), proj_pool[l].astype(BF16), proj_ssm[l].astype(BF16),
                       glu_w[l].astype(BF16), glu_b[l], w_out[l].astype(BF16),
                       ln1_g[l], ln1_b[l], alpha, tl["merge"])
        h, hb = _mlp(hb, h, w_up[l].astype(BF16), w_down[l].astype(BF16), ln2_g[l], ln2_b[l],
                     alpha, tl["mlp"], tl["ff"])
    return h.reshape(bsz, seq, d)
```

```python
import functools
import math

import jax
import jax.numpy as jnp
from jax import lax
from jax.experimental import pallas as pl
from jax.experimental.pallas import tpu as pltpu

F32 = jnp.float32
BF16 = jnp.bfloat16

D_MODEL = 2048
ATT_HEADS = 8
ATT_QK_DIM = 64
HEAD_W = 2 * ATT_QK_DIM
ATT_WIDTH = ATT_HEADS * HEAD_W
ROPE_THETA = 500000.0
ROT_DIM = ATT_QK_DIM // 4
ROT_HALF = ROT_DIM // 2
POOL_WINDOWS = (2, 4, 8, 16)
POOL_CH = 128
POOL_WIDTH = len(POOL_WINDOWS) * POOL_CH
POOL_HALO = 16
SSM_GROUP_CH = 16
SSM_WIDTH = 512
SSM_GROUPS = SSM_WIDTH // SSM_GROUP_CH
SSM_STATE = 64
D_FF = 4 * D_MODEL
LN_EPS = 1e-5
K_OFF = ATT_WIDTH
V_OFF = 2 * ATT_WIDTH
POOL_OFF = 3 * ATT_WIDTH
SSM_OFF = POOL_OFF + POOL_WIDTH
GATE_OFF = SSM_OFF + SSM_WIDTH
GATE_COLS = 3 * D_MODEL

LANES = 128
BF16_ROWS = 16
SSM_SLABS = SSM_WIDTH // LANES
SLAB_GROUPS = LANES // SSM_GROUP_CH
SLAB_STATE = SLAB_GROUPS * SSM_STATE
SSM_CHUNK = 16
SSM_ROW = SSM_CHUNK * LANES
SSM_CROW = SSM_CHUNK * SSM_GROUP_CH
VT_ROWS = HEAD_W + BF16_ROWS
VMEM_LIMIT = 56 * 1024 * 1024

NEG_BIG = -1e30
LOG2E = math.log2(math.e)


def _tiles(seq):
    return dict(
        ln=min(512, seq),
        proj=min(1024, seq),
        att=min(512, seq),
        pool=min(1024, seq),
        ssm=min(256, seq // SSM_CHUNK),
        merge=min(256, seq),
        mlp=min(512, seq),
        ff=512,
    )


def _cparams(sem):
    return pltpu.CompilerParams(dimension_semantics=sem, vmem_limit_bytes=VMEM_LIMIT)


def _ln_rows(z, g, b):
    mu = jnp.mean(z, axis=-1, keepdims=True)
    zc = z - mu
    var = jnp.mean(zc * zc, axis=-1, keepdims=True)
    return zc * lax.rsqrt(var + LN_EPS) * g + b


def _ln_kernel(x_ref, g_ref, b_ref, h_ref, hb_ref):
    y = _ln_rows(x_ref[...], g_ref[...], b_ref[...])
    h_ref[...] = y
    hb_ref[...] = y.astype(BF16)


def _layer_norm(x2d, g, b, tm):
    s, d = x2d.shape
    row = pl.BlockSpec((tm, d), lambda i: (i, 0))
    vec = pl.BlockSpec((1, d), lambda i: (0, 0))
    return pl.pallas_call(
        _ln_kernel,
        grid=(s // tm,),
        in_specs=[row, vec, vec],
        out_specs=[row, row],
        out_shape=[jax.ShapeDtypeStruct((s, d), F32), jax.ShapeDtypeStruct((s, d), BF16)],
        compiler_params=_cparams(("parallel",)),
        name="ln_in",
    )(x2d, g.reshape(1, d), b.reshape(1, d))


def _rotary_coeffs(pos_ref, invf_ref, scale):
    ang = pos_ref[...].astype(F32) * invf_ref[...]
    cos, sin = jnp.cos(ang), jnp.sin(ang)
    d = lax.broadcasted_iota(jnp.int32, ang.shape, 1) % ATT_QK_DIM
    c_same = jnp.where(d < ROT_DIM, cos, 1.0) * scale
    c_up = jnp.where(d < ROT_HALF, -sin, 0.0) * scale
    c_dn = jnp.where((d >= ROT_HALF) & (d < ROT_DIM), sin, 0.0) * scale
    return c_same, c_up, c_dn


def _rotate(xg, coeffs):
    c_same, c_up, c_dn = coeffs
    up = pltpu.roll(xg, LANES - ROT_HALF, axis=1)
    dn = pltpu.roll(xg, ROT_HALF, axis=1)
    return xg * c_same + up * c_up + dn * c_dn


def _proj_q_kernel(x_ref, w_ref, pos_ref, invf_ref, o_ref, *, tq):
    acc = jnp.dot(x_ref[...], w_ref[...], preferred_element_type=F32)
    coeffs = _rotary_coeffs(pos_ref, invf_ref, LOG2E / math.sqrt(ATT_QK_DIM))
    for c in range(acc.shape[1] // LANES):
        rot = _rotate(acc[:, c * LANES:(c + 1) * LANES], coeffs)
        for b in range(o_ref.shape[0]):
            o_ref[b, c * LANES:(c + 1) * LANES, :] = rot[b * tq:(b + 1) * tq, :].T.astype(o_ref.dtype)


def _proj_k_kernel(x_ref, w_ref, pos_ref, invf_ref, o_ref):
    acc = jnp.dot(x_ref[...], w_ref[...], preferred_element_type=F32)
    coeffs = _rotary_coeffs(pos_ref, invf_ref, 1.0)
    for c in range(acc.shape[1] // LANES):
        cols = slice(c * LANES, (c + 1) * LANES)
        o_ref[:, cols] = _rotate(acc[:, cols], coeffs).astype(o_ref.dtype)


def _proj_v_kernel(x_ref, w_ref, o_ref, *, tk):
    acc = jnp.dot(x_ref[...], w_ref[...], preferred_element_type=F32)
    ones = jnp.ones((BF16_ROWS, tk), o_ref.dtype)
    for b in range(o_ref.shape[0]):
        for hh in range(o_ref.shape[1]):
            blk = acc[b * tk:(b + 1) * tk, hh * HEAD_W:(hh + 1) * HEAD_W]
            o_ref[b, hh, 0:HEAD_W, :] = blk.T.astype(o_ref.dtype)
            o_ref[b, hh, HEAD_W:, :] = ones


def _proj_plain_kernel(x_ref, w_ref, o_ref):
    o_ref[...] = jnp.dot(x_ref[...], w_ref[...], preferred_element_type=F32).astype(o_ref.dtype)


def _proj_slab_kernel(x_ref, w_ref, o_ref):
    acc = jnp.dot(x_ref[...], w_ref[...], preferred_element_type=F32)
    for c in range(o_ref.shape[0]):
        o_ref[c] = acc[:, c * LANES:(c + 1) * LANES]


def _proj_gate_kernel(x_ref, w_ref, b_ref, o_ref):
    acc = jnp.dot(x_ref[...], w_ref[...], preferred_element_type=F32)
    o_ref[...] = jax.nn.sigmoid(acc + b_ref[...])


def _in_projections(hb, w_in_b, b_gate, pos_col, invf, tm, tq):
    s, d = hb.shape
    tn = 512
    nb = s // tq
    x_spec = pl.BlockSpec((tm, d), lambda i, j: (i, 0))

    def w_spec(off):
        return pl.BlockSpec((d, tn), lambda i, j: (0, j + off // tn))

    out_tile = pl.BlockSpec((tm, tn), lambda i, j: (i, j))
    pos_spec = pl.BlockSpec((tm, 1), lambda i, j: (i, 0))
    invf_spec = pl.BlockSpec((1, LANES), lambda i, j: (0, 0))
    sem = _cparams(("parallel", "arbitrary"))

    qt = pl.pallas_call(
        functools.partial(_proj_q_kernel, tq=tq),
        grid=(s // tm, ATT_WIDTH // tn),
        in_specs=[x_spec, w_spec(0), pos_spec, invf_spec],
        out_specs=pl.BlockSpec((tm // tq, tn, tq), lambda i, j: (i, j, 0)),
        out_shape=jax.ShapeDtypeStruct((nb, ATT_WIDTH, tq), BF16),
        compiler_params=sem, name="proj_q",
    )(hb, w_in_b, pos_col, invf)

    k = pl.pallas_call(
        _proj_k_kernel,
        grid=(s // tm, ATT_WIDTH // tn),
        in_specs=[x_spec, w_spec(K_OFF), pos_spec, invf_spec],
        out_specs=out_tile,
        out_shape=jax.ShapeDtypeStruct((s, ATT_WIDTH), BF16),
        compiler_params=sem, name="proj_k",
    )(hb, w_in_b, pos_col, invf)

    vt = pl.pallas_call(
        functools.partial(_proj_v_kernel, tk=tq),
        grid=(s // tm, ATT_WIDTH // tn),
        in_specs=[x_spec, w_spec(V_OFF)],
        out_specs=pl.BlockSpec((tm // tq, tn // HEAD_W, VT_ROWS, tq), lambda i, j: (i, j, 0, 0)),
        out_shape=jax.ShapeDtypeStruct((nb, ATT_HEADS, VT_ROWS, tq), BF16),
        compiler_params=sem, name="proj_v",
    )(hb, w_in_b)

    u_pool = pl.pallas_call(
        _proj_plain_kernel,
        grid=(s // tm, POOL_WIDTH // tn),
        in_specs=[x_spec, w_spec(POOL_OFF)],
        out_specs=out_tile,
        out_shape=jax.ShapeDtypeStruct((s, POOL_WIDTH), F32),
        compiler_params=sem, name="proj_pool_in",
    )(hb, w_in_b)

    u_ssm = pl.pallas_call(
        _proj_slab_kernel,
        grid=(s // tm, SSM_WIDTH // tn),
        in_specs=[x_spec, w_spec(SSM_OFF)],
        out_specs=pl.BlockSpec((SSM_SLABS, tm, LANES), lambda i, j: (0, i, 0)),
        out_shape=jax.ShapeDtypeStruct((SSM_SLABS, s, LANES), F32),
        compiler_params=sem, name="proj_ssm_in",
    )(hb, w_in_b)

    gates = pl.pallas_call(
        _proj_gate_kernel,
        grid=(s // tm, GATE_COLS // tn),
        in_specs=[x_spec, w_spec(GATE_OFF), pl.BlockSpec((1, tn), lambda i, j: (0, j))],
        out_specs=out_tile,
        out_shape=jax.ShapeDtypeStruct((s, GATE_COLS), F32),
        compiler_params=sem, name="proj_gates",
    )(hb, w_in_b, b_gate.reshape(1, GATE_COLS))
    return qt, k, vt, u_pool, u_ssm, gates


def _attn_kernel(lam_ref, qt_ref, k_ref, vt_ref, g_ref, o_ref,
                 sa1, sa2, sb1, sb2, ba1, ba2, bb1, bb2, m1_ref, a1_ref, m2_ref, a2_ref, *, tq):
    i = pl.program_id(1)
    qt = qt_ref[0]
    comp = lax.broadcasted_iota(jnp.int32, qt.shape, 0) < ATT_QK_DIM
    zero = jnp.zeros_like(qt)
    q1t = jnp.where(comp, qt, zero)
    q2t = jnp.where(comp, zero, qt)

    for m_ref, a_ref in ((m1_ref, a1_ref), (m2_ref, a2_ref)):
        m_ref[...] = jnp.full(m_ref.shape, NEG_BIG, F32)
        a_ref[...] = jnp.zeros(a_ref.shape, F32)

    buf_a = ((sa1, ba1), (sa2, ba2))
    buf_b = ((sb1, bb1), (sb2, bb2))

    def scores(j, buf, masked):
        kb = k_ref[pl.ds(pl.multiple_of(j * tq, tq), tq), :]
        for qct, (s_ref, bm_ref) in zip((q1t, q2t), buf):
            s = jnp.dot(kb, qct, preferred_element_type=F32)
            if masked:
                keep = (lax.broadcasted_iota(jnp.int32, s.shape, 0)
                        <= lax.broadcasted_iota(jnp.int32, s.shape, 1))
                s = jnp.where(keep, s, NEG_BIG)
            s_ref[...] = s
            bm_ref[...] = jnp.max(s, axis=0, keepdims=True)

    def consume(j, buf):
        vb = vt_ref[j, 0]
        for (s_ref, bm_ref), m_ref, a_ref in zip(buf, (m1_ref, m2_ref), (a1_ref, a2_ref)):
            m_old = m_ref[...]
            m_new = jnp.maximum(m_old, bm_ref[...])
            alpha = jnp.exp2(m_old - m_new)
            p = jnp.exp2(s_ref[...] - m_new).astype(BF16)
            a_ref[...] = alpha * a_ref[...] + jnp.dot(vb, p, preferred_element_type=F32)
            m_ref[...] = m_new

    @pl.when(i == 0)
    def _():
        scores(0, buf_a, True)
        consume(0, buf_a)

    @pl.when(i > 0)
    def _():
        scores(0, buf_a, False)
        npairs = lax.shift_right_logical(i - 1, 1)

        def pair(t, carry):
            scores(2 * t + 1, buf_b, False)
            consume(2 * t, buf_a)
            scores(2 * t + 2, buf_a, False)
            consume(2 * t + 1, buf_b)
            return carry

        lax.fori_loop(0, npairs, pair, 0)

        @pl.when(i % 2 == 1)
        def _():
            scores(i, buf_b, True)
            consume(i - 1, buf_a)
            consume(i, buf_b)

        @pl.when(i % 2 == 0)
        def _():
            scores(i - 1, buf_b, False)
            consume(i - 2, buf_a)
            scores(i, buf_a, True)
            consume(i - 1, buf_b)
            consume(i, buf_a)

    lam = lam_ref[0, 0]
    o = (a1_ref[0:HEAD_W, :] / a1_ref[HEAD_W:HEAD_W + 1, :]
         - lam * (a2_ref[0:HEAD_W, :] / a2_ref[HEAD_W:HEAD_W + 1, :]))
    o = o * lax.rsqrt(jnp.mean(o * o, axis=0, keepdims=True) + LN_EPS) * g_ref[...]
    o_ref[...] = o.T.astype(o_ref.dtype)


def _attention(qt, k, vt, lam, g_col, tq):
    s = k.shape[0]
    nb = s // tq
    score = pltpu.VMEM((tq, tq), F32)
    stat = pltpu.VMEM((1, tq), F32)
    accum = pltpu.VMEM((VT_ROWS, tq), F32)
    return pl.pallas_call(
        functools.partial(_attn_kernel, tq=tq),
        grid=(ATT_HEADS, nb),
        in_specs=[
            pl.BlockSpec(memory_space=pltpu.SMEM),
            pl.BlockSpec((1, HEAD_W, tq), lambda h, i: (i, h, 0)),
            pl.BlockSpec((s, HEAD_W), lambda h, i: (0, h)),
            pl.BlockSpec((nb, 1, VT_ROWS, tq), lambda h, i: (0, h, 0, 0)),
            pl.BlockSpec((HEAD_W, 1), lambda h, i: (0, 0)),
        ],
        out_specs=pl.BlockSpec((tq, HEAD_W), lambda h, i: (i, h)),
        out_shape=jax.ShapeDtypeStruct((s, ATT_WIDTH), BF16),
        scratch_shapes=[score, score, score, score, stat, stat, stat, stat, stat, accum, stat, accum],
        compiler_params=_cparams(("parallel", "arbitrary")),
        name="diff_attention",
    )(lam, qt, k, vt, g_col)


def _pool_kernel(u_ref, halo_ref, w_ref, sc_ref, o_ref, buf_ref, *, tm):
    i = pl.program_id(0)
    halo = halo_ref[...]
    buf_ref[0:POOL_HALO, :] = jnp.where(i > 0, halo, jnp.zeros_like(halo))
    buf_ref[POOL_HALO:, :] = u_ref[...]
    t = i * tm + lax.broadcasted_iota(jnp.int32, (tm, 1), 0)
    for g, w in enumerate(POOL_WINDOWS):
        cols = slice(g * POOL_CH, (g + 1) * POOL_CH)
        acc = buf_ref[POOL_HALO:, cols]
        for back in range(1, w):
            acc = acc + buf_ref[POOL_HALO - back:POOL_HALO - back + tm, cols]
        cnt = jnp.minimum(t + 1, w).astype(F32)
        pooled = acc / cnt - buf_ref[POOL_HALO:, cols]
        mixed = jnp.dot(pooled.astype(BF16), w_ref[g], preferred_element_type=F32)
        o_ref[:, cols] = (mixed * sc_ref[:, cols]).astype(o_ref.dtype)


def _pool(u_pool, pool_w_b, pool_scale, tm):
    s = u_pool.shape[0]
    return pl.pallas_call(
        functools.partial(_pool_kernel, tm=tm),
        grid=(s // tm,),
        in_specs=[
            pl.BlockSpec((tm, POOL_WIDTH), lambda i: (i, 0)),
            pl.BlockSpec((POOL_HALO, POOL_WIDTH),
                         lambda i: (jnp.maximum(i * (tm // POOL_HALO) - 1, 0), 0)),
            pl.BlockSpec((len(POOL_WINDOWS), POOL_CH, POOL_CH), lambda i: (0, 0, 0)),
            pl.BlockSpec((1, POOL_WIDTH), lambda i: (0, 0)),
        ],
        out_specs=pl.BlockSpec((tm, POOL_WIDTH), lambda i: (i, 0)),
        out_shape=jax.ShapeDtypeStruct((s, POOL_WIDTH), BF16),
        scratch_shapes=[pltpu.VMEM((tm + POOL_HALO, POOL_WIDTH), F32)],
        compiler_params=_cparams(("parallel",)),
        name="pool",
    )(u_pool, u_pool, pool_w_b, pool_scale.reshape(1, POOL_WIDTH))


def _ssm_matrices(a_re, a_im, log_dt, b_re, b_im, c_re, c_im, d_skip):
    hp = lax.Precision.HIGHEST
    t_len = SSM_CHUNK
    ar, ai = a_re.astype(F32), a_im.astype(F32)
    dt = jnp.exp(log_dt.astype(F32))[:, None]
    mag = jnp.exp(ar * dt)
    ab_re, ab_im = mag * jnp.cos(ai * dt), mag * jnp.sin(ai * dt)
    den = ar * ar + ai * ai
    nr, ni = ab_re - 1.0, ab_im
    f_re = (nr * ar + ni * ai) / den
    f_im = (ni * ar - nr * ai) / den
    br, bi = b_re.astype(F32), b_im.astype(F32)
    bb_re = f_re[..., None] * br - f_im[..., None] * bi
    bb_im = f_re[..., None] * bi + f_im[..., None] * br
    kk = jnp.arange(t_len + 1, dtype=F32)[:, None, None]
    pmag = jnp.exp(ar * dt * kk)
    pw_re, pw_im = pmag * jnp.cos(ai * dt * kk), pmag * jnp.sin(ai * dt * kk)
    wb_re = pw_re[..., None] * bb_re - pw_im[..., None] * bb_im
    wb_im = pw_re[..., None] * bb_im + pw_im[..., None] * bb_re
    cr, ci = c_re.astype(F32), c_im.astype(F32)
    kmat = (jnp.einsum("kgpi,gop->gkio", wb_re, cr, precision=hp)
            - jnp.einsum("kgpi,gop->gkio", wb_im, ci, precision=hp))
    h = SSM_GROUP_CH

    lag = jnp.arange(t_len)[None, :] - jnp.arange(t_len)[:, None]
    kst = kmat[:, jnp.clip(lag, 0, t_len)] * (lag >= 0)[None, :, :, None, None].astype(F32)
    kc = kst.reshape(SSM_SLABS, SLAB_GROUPS, t_len, t_len, h, h).transpose(0, 2, 1, 4, 3, 5)
    kc = kc.reshape(SSM_SLABS, SSM_ROW, SSM_CROW)

    rev = t_len - 1 - jnp.arange(t_len)
    pst = jnp.stack([wb_re[rev], wb_im[rev]], axis=0)
    pc = pst.reshape(2, t_len, SSM_SLABS, SLAB_GROUPS, SSM_STATE, h).transpose(2, 1, 3, 5, 0, 4)
    pc = pc.reshape(SSM_SLABS, SSM_ROW, 2 * SSM_STATE)

    pr, pi = pw_re[1:], pw_im[1:]
    q_re = pr[:, :, None, :] * cr[None] - pi[:, :, None, :] * ci[None]
    q_im = -pi[:, :, None, :] * cr[None] - pr[:, :, None, :] * ci[None]
    qst = jnp.stack([q_re, q_im], axis=0).reshape(2, t_len, SSM_SLABS, SLAB_GROUPS, h, SSM_STATE)
    qc = qst.transpose(2, 0, 3, 5, 1, 4).reshape(SSM_SLABS, 2 * SLAB_STATE, SSM_CROW)

    a_t = jnp.stack([pw_re[t_len], pw_im[t_len]], axis=0)
    a_t = a_t.reshape(2, SSM_SLABS, SLAB_STATE).transpose(1, 0, 2)
    d_row = jnp.tile(d_skip.astype(F32).reshape(SSM_SLABS, 1, LANES), (1, 1, t_len))

    col = jnp.arange(SSM_ROW)
    rep_k = (jnp.arange(SSM_CROW)[:, None]
             == (col // LANES) * h + col % h).astype(BF16)
    scol = jnp.arange(2 * SLAB_STATE)
    rep_p = (jnp.arange(2 * SSM_STATE)[:, None]
             == (scol // SLAB_STATE) * SSM_STATE + scol % SSM_STATE).astype(BF16)
    return kc.astype(BF16), pc.astype(BF16), qc.astype(BF16), rep_k, rep_p, a_t, d_row


def _gelu_tanh(x):
    return x * (0.5 * (1.0 + jnp.tanh(math.sqrt(2.0 / math.pi) * (x + 0.044715 * (x * x * x)))))


def _lane_group(idx):
    return (idx >> 4) & (SLAB_GROUPS - 1)


def _state_group(idx):
    return (idx >> 6) & (SLAB_GROUPS - 1)


def _expand_block_diag(compact, rep_ref, out_ref, row_group, col_group, step):
    rows = compact.shape[0]
    for c0 in range(0, out_ref.shape[1], step):
        full = jnp.dot(compact, rep_ref[:, c0:c0 + step], preferred_element_type=F32)
        rg = row_group(lax.broadcasted_iota(jnp.int32, (rows, step), 0))
        cg = col_group(lax.broadcasted_iota(jnp.int32, (rows, step), 1) + c0)
        out_ref[:, c0:c0 + step] = jnp.where(rg == cg, full, 0.0).astype(out_ref.dtype)


def _ssm_kernel(u_ref, kc_ref, pc_ref, qc_ref, repk_ref, repp_ref, at_ref, d_ref, o_ref,
                m_ref, p_ref, q_ref, xloc_ref, xprev_ref, st_ref, *, cm):
    @pl.when(pl.program_id(1) == 0)
    def _():
        st_ref[...] = jnp.zeros(st_ref.shape, F32)
        _expand_block_diag(kc_ref[0], repk_ref, m_ref, _lane_group, _lane_group, 512)
        _expand_block_diag(pc_ref[0], repp_ref, p_ref, _lane_group, _state_group, 512)
        _expand_block_diag(qc_ref[0], repk_ref, q_ref, _state_group, _lane_group, 512)

    u = u_ref[0]
    ub = u.astype(BF16)
    xloc_ref[...] = jnp.dot(ub, p_ref[...], preferred_element_type=F32)
    a_r = at_ref[0, 0:1, :]
    a_i = at_ref[0, 1:2, :]

    def step(c, carry):
        xr, xi = carry
        xprev_ref[pl.ds(c, 1), 0:SLAB_STATE] = xr
        xprev_ref[pl.ds(c, 1), SLAB_STATE:] = xi
        br = xloc_ref[pl.ds(c, 1), 0:SLAB_STATE]
        bi = xloc_ref[pl.ds(c, 1), SLAB_STATE:]
        return a_r * xr - a_i * xi + br, a_r * xi + a_i * xr + bi

    xr, xi = lax.fori_loop(0, cm, step, (st_ref[0:1, :], st_ref[1:2, :]))
    st_ref[0:1, :] = xr
    st_ref[1:2, :] = xi

    y = jnp.dot(ub, m_ref[...], preferred_element_type=F32)
    y = y + jnp.dot(xprev_ref[...].astype(BF16), q_ref[...], preferred_element_type=F32)
    y = y + d_ref[0] * u
    o_ref[0] = _gelu_tanh(y)


def _ssm(u_slabs, mats, cm):
    kc, pc, qc, rep_k, rep_p, a_t, d_row = mats
    nslab, s, _ = u_slabs.shape
    nchunk = s // SSM_CHUNK
    u_rows = u_slabs.reshape(nslab, nchunk, SSM_ROW)
    rows = pl.BlockSpec((1, cm, SSM_ROW), lambda j, c: (j, c, 0))

    def per_slab(shape):
        return pl.BlockSpec((1,) + shape, lambda j, c: (j, 0, 0))

    def whole(shape):
        return pl.BlockSpec(shape, lambda j, c: (0, 0))

    y = pl.pallas_call(
        functools.partial(_ssm_kernel, cm=cm),
        grid=(nslab, nchunk // cm),
        in_specs=[rows, per_slab((SSM_ROW, SSM_CROW)), per_slab((SSM_ROW, 2 * SSM_STATE)),
                  per_slab((2 * SLAB_STATE, SSM_CROW)), whole((SSM_CROW, SSM_ROW)),
                  whole((2 * SSM_STATE, 2 * SLAB_STATE)), per_slab((2, SLAB_STATE)), per_slab((1, SSM_ROW))],
        out_specs=rows,
        out_shape=jax.ShapeDtypeStruct((nslab, nchunk, SSM_ROW), F32),
        scratch_shapes=[pltpu.VMEM((SSM_ROW, SSM_ROW), BF16), pltpu.VMEM((SSM_ROW, 2 * SLAB_STATE), BF16),
                        pltpu.VMEM((2 * SLAB_STATE, SSM_ROW), BF16),
                        pltpu.VMEM((cm, 2 * SLAB_STATE), F32), pltpu.VMEM((cm, 2 * SLAB_STATE), F32),
                        pltpu.VMEM((2, SLAB_STATE), F32)],
        compiler_params=_cparams(("parallel", "arbitrary")),
        name="s5_ssm",
    )(u_rows, kc, pc, qc, rep_k, rep_p, a_t, d_row)
    return y.reshape(nslab, s, LANES)


def _merge_kernel(ya_ref, yp_ref, ys_ref, g_ref, h_ref, wa_ref, wp_ref, ws_ref, wg_ref, bg_ref,
                  wo_ref, lg_ref, lb_ref, o_ref, ob_ref, *, alpha):
    ys = jnp.concatenate([ys_ref[c] for c in range(SSM_SLABS)], axis=1)
    z = jnp.dot(ys.astype(BF16), wg_ref[...], preferred_element_type=F32) + bg_ref[...]
    y_ssm = ys * jax.nn.sigmoid(z)
    merged = g_ref[:, 0:D_MODEL] * jnp.dot(ya_ref[...], wa_ref[...], preferred_element_type=F32)
    merged = merged + g_ref[:, D_MODEL:2 * D_MODEL] * jnp.dot(
        yp_ref[...], wp_ref[...], preferred_element_type=F32)
    merged = merged + g_ref[:, 2 * D_MODEL:] * jnp.dot(
        y_ssm.astype(BF16), ws_ref[...], preferred_element_type=F32)
    z = alpha * h_ref[...] + jnp.dot(merged.astype(BF16), wo_ref[...], preferred_element_type=F32)
    y = _ln_rows(z, lg_ref[...], lb_ref[...])
    o_ref[...] = y
    ob_ref[...] = y.astype(BF16)


def _merge(y_attn, y_pool, y_s, gates, h, wa, wp, ws, wg, bg, wo, ln_g, ln_b, alpha, tm):
    s, d = h.shape

    def rows(width):
        return pl.BlockSpec((tm, width), lambda i: (i, 0))

    def whole(shape):
        return pl.BlockSpec(shape, lambda i: (0,) * len(shape), pipeline_mode=pl.Buffered(1))

    return pl.pallas_call(
        functools.partial(_merge_kernel, alpha=alpha),
        grid=(s // tm,),
        in_specs=[rows(ATT_WIDTH), rows(POOL_WIDTH),
                  pl.BlockSpec((SSM_SLABS, tm, LANES), lambda i: (0, i, 0)),
                  rows(GATE_COLS), rows(d),
                  whole((ATT_WIDTH, d)), whole((POOL_WIDTH, d)), whole((SSM_WIDTH, d)),
                  whole((SSM_WIDTH, SSM_WIDTH)), whole((1, SSM_WIDTH)), whole((d, d)),
                  whole((1, d)), whole((1, d))],
        out_specs=[rows(d), rows(d)],
        out_shape=[jax.ShapeDtypeStruct((s, d), F32), jax.ShapeDtypeStruct((s, d), BF16)],
        compiler_params=_cparams(("parallel",)),
        name="merge_out_ln",
    )(y_attn, y_pool, y_s, gates, h, wa, wp, ws, wg, bg.reshape(1, SSM_WIDTH), wo,
      ln_g.reshape(1, d), ln_b.reshape(1, d))


def _mlp_kernel(hb_ref, h_ref, wu_ref, wd_ref, lg_ref, lb_ref, o_ref, ob_ref, acc_ref, *, alpha):
    f = pl.program_id(1)

    @pl.when(f == 0)
    def _():
        acc_ref[...] = jnp.zeros(acc_ref.shape, F32)

    up = jnp.dot(hb_ref[...], wu_ref[...], preferred_element_type=F32)
    r = jnp.maximum(up, 0.0)
    acc_ref[...] += jnp.dot((r * r).astype(BF16), wd_ref[...], preferred_element_type=F32)

    @pl.when(f == pl.num_programs(1) - 1)
    def _():
        y = _ln_rows(alpha * h_ref[...] + acc_ref[...], lg_ref[...], lb_ref[...])
        o_ref[...] = y
        ob_ref[...] = y.astype(BF16)


def _mlp(hb, h, w_up_b, w_down_b, ln_g, ln_b, alpha, tm, tf):
    s, d = h.shape
    rows = pl.BlockSpec((tm, d), lambda i, f: (i, 0))
    vec = pl.BlockSpec((1, d), lambda i, f: (0, 0))
    return pl.pallas_call(
        functools.partial(_mlp_kernel, alpha=alpha),
        grid=(s // tm, D_FF // tf),
        in_specs=[rows, rows,
                  pl.BlockSpec((d, tf), lambda i, f: (0, f)),
                  pl.BlockSpec((tf, d), lambda i, f: (f, 0)),
                  vec, vec],
        out_specs=[rows, rows],
        out_shape=[jax.ShapeDtypeStruct((s, d), F32), jax.ShapeDtypeStruct((s, d), BF16)],
        scratch_shapes=[pltpu.VMEM((tm, d), F32)],
        compiler_params=_cparams(("parallel", "arbitrary")),
        name="mlp_ln",
    )(hb, h, w_up_b, w_down_b, ln_g.reshape(1, d), ln_b.reshape(1, d))


def kernel(x, positions, ln_in_g, ln_in_b, w_in, b_gate, lam_q1, lam_k1, lam_q2, lam_k2, subln_g, pool_w, pool_scale, ssm_a_re, ssm_a_im, ssm_log_dt, ssm_b_re, ssm_b_im, ssm_c_re, ssm_c_im, ssm_d, glu_w, glu_b, proj_attn, proj_pool, proj_ssm, w_out, ln1_g, ln1_b, w_up, w_down, ln2_g, ln2_b):
    bsz, seq, d = x.shape
    assert bsz == 1 and d == D_MODEL
    depth = w_in.shape[0]
    alpha = (2.0 * depth) ** 0.25
    tl = _tiles(seq)

    h, hb = _layer_norm(x.reshape(seq, d), ln_in_g, ln_in_b, tl["ln"])
    pos_col = positions.reshape(seq, 1)
    inv_freq = ROPE_THETA ** (-jnp.arange(0, ROT_DIM, 2, dtype=F32) / ROT_DIM)
    invf = jnp.tile(inv_freq, LANES // ROT_HALF).reshape(1, LANES)

    for l in range(depth):
        lam_init = 0.8 - 0.6 * math.exp(-0.3 * l)
        lam = (jnp.exp(jnp.sum(lam_q1[l].astype(F32) * lam_k1[l].astype(F32)))
               - jnp.exp(jnp.sum(lam_q2[l].astype(F32) * lam_k2[l].astype(F32))) + lam_init)
        g_col = (subln_g[l].astype(F32) * (1.0 - lam_init)).reshape(HEAD_W, 1)

        qt, k, vt, u_pool, u_ssm, gates = _in_projections(
            hb, w_in[l].astype(BF16), b_gate[l], pos_col, invf, tl["proj"], tl["att"])
        y_attn = _attention(qt, k, vt, lam.reshape(1, 1), g_col, tl["att"])
        y_pool = _pool(u_pool, pool_w[l].astype(BF16), pool_scale[l], tl["pool"])
        mats = _ssm_matrices(ssm_a_re[l], ssm_a_im[l], ssm_log_dt[l], ssm_b_re[l], ssm_b_im[l],
                             ssm_c_re[l], ssm_c_im[l], ssm_d[l])
        y_s = _ssm(u_ssm, mats, tl["ssm"])
        h, hb = _merge(y_attn, y_pool, y_s, gates, h,
                       proj_attn[l].astype(BF16), proj_pool[l].astype(BF16), proj_ssm[l].astype(BF16),
                       glu_w[l].astype(BF16), glu_b[l], w_out[l].astype(BF16),
                       ln1_g[l], ln1_b[l], alpha, tl["merge"])
        h, hb = _mlp(hb, h, w_up[l].astype(BF16), w_down[l].astype(BF16), ln2_g[l], ln2_b[l],
                     alpha, tl["mlp"], tl["ff"])
    return h.reshape(bsz, seq, d)
```

```python
import functools
import math

import jax
import jax.numpy as jnp
from jax import lax
from jax.experimental import pallas as pl
from jax.experimental.pallas import tpu as pltpu

F32 = jnp.float32
BF16 = jnp.bfloat16

D_MODEL = 2048
ATT_HEADS = 8
ATT_QK_DIM = 64
HEAD_W = 2 * ATT_QK_DIM
ATT_WIDTH = ATT_HEADS * HEAD_W
ROPE_THETA = 500000.0
ROT_DIM = ATT_QK_DIM // 4
ROT_HALF = ROT_DIM // 2
POOL_WINDOWS = (2, 4, 8, 16)
POOL_CH = 128
POOL_WIDTH = len(POOL_WINDOWS) * POOL_CH
POOL_HALO = 16
SSM_GROUP_CH = 16
SSM_WIDTH = 512
SSM_GROUPS = SSM_WIDTH // SSM_GROUP_CH
SSM_STATE = 64
D_FF = 4 * D_MODEL
LN_EPS = 1e-5
K_OFF = ATT_WIDTH
V_OFF = 2 * ATT_WIDTH
POOL_OFF = 3 * ATT_WIDTH
SSM_OFF = POOL_OFF + POOL_WIDTH
GATE_OFF = SSM_OFF + SSM_WIDTH
GATE_COLS = 3 * D_MODEL

LANES = 128
BF16_ROWS = 16
MXU_COLS = 256
GATE_CHUNK = MXU_COLS
GATE_TILE = 1024
SSM_SLABS = SSM_WIDTH // LANES
SLAB_GROUPS = LANES // SSM_GROUP_CH
SLAB_STATE = SLAB_GROUPS * SSM_STATE
SSM_CHUNK = 16
SSM_ROW = SSM_CHUNK * LANES
SSM_CROW = SSM_CHUNK * SSM_GROUP_CH
VT_ROWS = HEAD_W + BF16_ROWS
VMEM_LIMIT = 56 * 1024 * 1024

NEG_BIG = -1e30
LOG2E = math.log2(math.e)


def _tiles(seq):
    tiles = _tile_table(seq)
    assert all(seq % tiles[name] == 0 for name in ("ln", "proj", "att", "pool", "merge", "mlp"))
    assert (seq // SSM_CHUNK) % tiles["ssm"] == 0 and tiles["proj"] % tiles["att"] == 0
    return tiles


def _tile_table(seq):
    return dict(
        ln=min(512, seq),
        proj=min(1024, seq),
        att=min(1024, seq),
        pool=min(1024, seq),
        ssm=min(256, seq // SSM_CHUNK),
        merge=min(256, seq),
        mlp=min(1024, seq),
        ff=512,
    )


def _cparams(sem):
    return pltpu.CompilerParams(dimension_semantics=sem, vmem_limit_bytes=VMEM_LIMIT)


def _ln_rows(z, g, b):
    mu = jnp.mean(z, axis=-1, keepdims=True)
    zc = z - mu
    var = jnp.mean(zc * zc, axis=-1, keepdims=True)
    return zc * lax.rsqrt(var + LN_EPS) * g + b


def _ln_kernel(x_ref, g_ref, b_ref, h_ref, hb_ref):
    y = _ln_rows(x_ref[...], g_ref[...], b_ref[...])
    h_ref[...] = y
    hb_ref[...] = y.astype(BF16)


def _layer_norm(x2d, g, b, tm):
    s, d = x2d.shape
    row = pl.BlockSpec((tm, d), lambda i: (i, 0))
    vec = pl.BlockSpec((1, d), lambda i: (0, 0))
    return pl.pallas_call(
        _ln_kernel,
        grid=(s // tm,),
        in_specs=[row, vec, vec],
        out_specs=[row, row],
        out_shape=[jax.ShapeDtypeStruct((s, d), F32), jax.ShapeDtypeStruct((s, d), BF16)],
        compiler_params=_cparams(("parallel",)),
        name="ln_in",
    )(x2d, g.reshape(1, d), b.reshape(1, d))


def _rope_kernel(pos_ref, invf_ref, same_ref, up_ref, dn_ref):
    ang = pos_ref[...].astype(F32) * invf_ref[...]
    cos, sin = jnp.cos(ang), jnp.sin(ang)
    d = lax.broadcasted_iota(jnp.int32, ang.shape, 1) % ATT_QK_DIM
    same_ref[...] = jnp.where(d < ROT_DIM, cos, 1.0)
    up_ref[...] = jnp.where(d < ROT_HALF, -sin, 0.0)
    dn_ref[...] = jnp.where((d >= ROT_HALF) & (d < ROT_DIM), sin, 0.0)


def _rope_tables(pos_col, invf, tm):
    s = pos_col.shape[0]
    tab = pl.BlockSpec((tm, LANES), lambda i: (i, 0))
    return pl.pallas_call(
        _rope_kernel,
        grid=(s // tm,),
        in_specs=[pl.BlockSpec((tm, 1), lambda i: (i, 0)), pl.BlockSpec((1, LANES), lambda i: (0, 0))],
        out_specs=[tab, tab, tab],
        out_shape=[jax.ShapeDtypeStruct((s, LANES), F32)] * 3,
        compiler_params=_cparams(("parallel",)),
        name="rope_tables",
    )(pos_col, invf)


def _rotate(xg, same_ref, up_ref, dn_ref):
    up = pltpu.roll(xg, LANES - ROT_HALF, axis=1)
    dn = pltpu.roll(xg, ROT_HALF, axis=1)
    return xg * same_ref[...] + up * up_ref[...] + dn * dn_ref[...]


def _proj_q_kernel(x_ref, w_ref, same_ref, up_ref, dn_ref, o_ref, *, tq):
    acc = jnp.dot(x_ref[...], w_ref[...], preferred_element_type=F32)
    scale = LOG2E / math.sqrt(ATT_QK_DIM)
    for c in range(acc.shape[1] // LANES):
        rot = _rotate(acc[:, c * LANES:(c + 1) * LANES], same_ref, up_ref, dn_ref) * scale
        for b in range(o_ref.shape[0]):
            o_ref[b, c * LANES:(c + 1) * LANES, :] = rot[b * tq:(b + 1) * tq, :].T.astype(o_ref.dtype)


def _proj_k_kernel(x_ref, w_ref, same_ref, up_ref, dn_ref, o_ref):
    acc = jnp.dot(x_ref[...], w_ref[...], preferred_element_type=F32)
    for c in range(acc.shape[1] // LANES):
        cols = slice(c * LANES, (c + 1) * LANES)
        o_ref[:, cols] = _rotate(acc[:, cols], same_ref, up_ref, dn_ref).astype(o_ref.dtype)


def _proj_v_kernel(x_ref, w_ref, o_ref, *, tk):
    acc = jnp.dot(x_ref[...], w_ref[...], preferred_element_type=F32)
    ones = jnp.ones((BF16_ROWS, tk), o_ref.dtype)
    for b in range(o_ref.shape[0]):
        for hh in range(o_ref.shape[1]):
            blk = acc[b * tk:(b + 1) * tk, hh * HEAD_W:(hh + 1) * HEAD_W]
            o_ref[b, hh, 0:HEAD_W, :] = blk.T.astype(o_ref.dtype)
            o_ref[b, hh, HEAD_W:, :] = ones


def _proj_plain_kernel(x_ref, w_ref, o_ref):
    o_ref[...] = jnp.dot(x_ref[...], w_ref[...], preferred_element_type=F32).astype(o_ref.dtype)


def _proj_slab_kernel(x_ref, w_ref, o_ref):
    acc = jnp.dot(x_ref[...], w_ref[...], preferred_element_type=F32)
    for c in range(o_ref.shape[0]):
        o_ref[c] = acc[:, c * LANES:(c + 1) * LANES]


def _proj_gate_kernel(x_ref, w_ref, b_ref, o_ref):
    x = x_ref[...]
    for c in range(o_ref.shape[1] // GATE_CHUNK):
        cols = slice(c * GATE_CHUNK, (c + 1) * GATE_CHUNK)
        z = jnp.dot(x, w_ref[:, cols], preferred_element_type=F32) + b_ref[:, cols]
        o_ref[:, cols] = (0.5 * jnp.tanh(0.5 * z) + 0.5).astype(o_ref.dtype)


def _in_projections(hb, w_in_b, b_gate, rope, tm, tq):
    s, d = hb.shape
    tn = 512
    nb = s // tq
    x_spec = pl.BlockSpec((tm, d), lambda i, j: (i, 0))

    def w_spec(off):
        return pl.BlockSpec((d, tn), lambda i, j: (0, j + off // tn))

    out_tile = pl.BlockSpec((tm, tn), lambda i, j: (i, j))
    rope_specs = [pl.BlockSpec((tm, LANES), lambda i, j: (i, 0))] * 3
    sem = _cparams(("parallel", "arbitrary"))

    qt = pl.pallas_call(
        functools.partial(_proj_q_kernel, tq=tq),
        grid=(s // tm, ATT_WIDTH // tn),
        in_specs=[x_spec, w_spec(0)] + rope_specs,
        out_specs=pl.BlockSpec((tm // tq, tn, tq), lambda i, j: (i, j, 0)),
        out_shape=jax.ShapeDtypeStruct((nb, ATT_WIDTH, tq), BF16),
        compiler_params=sem, name="proj_q",
    )(hb, w_in_b, *rope)

    k = pl.pallas_call(
        _proj_k_kernel,
        grid=(s // tm, ATT_WIDTH // tn),
        in_specs=[x_spec, w_spec(K_OFF)] + rope_specs,
        out_specs=out_tile,
        out_shape=jax.ShapeDtypeStruct((s, ATT_WIDTH), BF16),
        compiler_params=sem, name="proj_k",
    )(hb, w_in_b, *rope)

    vt = pl.pallas_call(
        functools.partial(_proj_v_kernel, tk=tq),
        grid=(s // tm, ATT_WIDTH // tn),
        in_specs=[x_spec, w_spec(V_OFF)],
        out_specs=pl.BlockSpec((tm // tq, tn // HEAD_W, VT_ROWS, tq), lambda i, j: (i, j, 0, 0)),
        out_shape=jax.ShapeDtypeStruct((nb, ATT_HEADS, VT_ROWS, tq), BF16),
        compiler_params=sem, name="proj_v",
    )(hb, w_in_b)

    u_pool = pl.pallas_call(
        _proj_plain_kernel,
        grid=(s // tm, POOL_WIDTH // tn),
        in_specs=[x_spec, w_spec(POOL_OFF)],
        out_specs=out_tile,
        out_shape=jax.ShapeDtypeStruct((s, POOL_WIDTH), F32),
        compiler_params=sem, name="proj_pool_in",
    )(hb, w_in_b)

    u_ssm = pl.pallas_call(
        _proj_slab_kernel,
        grid=(s // tm, SSM_WIDTH // tn),
        in_specs=[x_spec, w_spec(SSM_OFF)],
        out_specs=pl.BlockSpec((SSM_SLABS, tm, LANES), lambda i, j: (0, i, 0)),
        out_shape=jax.ShapeDtypeStruct((SSM_SLABS, s, LANES), F32),
        compiler_params=sem, name="proj_ssm_in",
    )(hb, w_in_b)

    gates = pl.pallas_call(
        _proj_gate_kernel,
        grid=(s // tm, GATE_COLS // GATE_TILE),
        in_specs=[x_spec,
                  pl.BlockSpec((d, GATE_TILE), lambda i, j: (0, j + GATE_OFF // GATE_TILE)),
                  pl.BlockSpec((1, GATE_TILE), lambda i, j: (0, j))],
        out_specs=pl.BlockSpec((tm, GATE_TILE), lambda i, j: (i, j)),
        out_shape=jax.ShapeDtypeStruct((s, GATE_COLS), BF16),
        compiler_params=sem, name="proj_gates",
    )(hb, w_in_b, b_gate.reshape(1, GATE_COLS))
    return qt, k, vt, u_pool, u_ssm, gates


def _attn_kernel(lam_ref, qt_ref, k_ref, vt_ref, g_ref, o_ref,
                 sa1, sa2, sb1, sb2, ba1, ba2, bb1, bb2, m1_ref, a1_ref, m2_ref, a2_ref, *, tq):
    i = pl.program_id(1)
    qt = qt_ref[0]
    comp = lax.broadcasted_iota(jnp.int32, qt.shape, 0) < ATT_QK_DIM
    zero = jnp.zeros_like(qt)
    q1t = jnp.where(comp, qt, zero)
    q2t = jnp.where(comp, zero, qt)

    for m_ref, a_ref in ((m1_ref, a1_ref), (m2_ref, a2_ref)):
        m_ref[...] = jnp.full(m_ref.shape, NEG_BIG, F32)
        a_ref[...] = jnp.zeros(a_ref.shape, F32)

    buf_a = ((sa1, ba1), (sa2, ba2))
    buf_b = ((sb1, bb1), (sb2, bb2))

    def scores(j, buf, masked):
        kb = k_ref[pl.ds(pl.multiple_of(j * tq, tq), tq), :]
        for qct, (s_ref, bm_ref) in zip((q1t, q2t), buf):
            s = jnp.dot(kb, qct, preferred_element_type=F32)
            if masked:
                keep = (lax.broadcasted_iota(jnp.int32, s.shape, 0)
                        <= lax.broadcasted_iota(jnp.int32, s.shape, 1))
                s = jnp.where(keep, s, NEG_BIG)
            s_ref[...] = s
            bm_ref[...] = jnp.max(s, axis=0, keepdims=True)

    def consume(j, buf):
        vb = vt_ref[j, 0]
        for (s_ref, bm_ref), m_ref, a_ref in zip(buf, (m1_ref, m2_ref), (a1_ref, a2_ref)):
            m_old = m_ref[...]
            m_new = jnp.maximum(m_old, bm_ref[...])
            alpha = jnp.exp2(m_old - m_new)
            p = jnp.exp2(s_ref[...] - m_new).astype(BF16)
            a_ref[...] = alpha * a_ref[...] + jnp.dot(vb, p, preferred_element_type=F32)
            m_ref[...] = m_new

    @pl.when(i == 0)
    def _():
        scores(0, buf_a, True)
        consume(0, buf_a)

    @pl.when(i > 0)
    def _():
        scores(0, buf_a, False)
        npairs = lax.shift_right_logical(i - 1, 1)

        def pair(t, carry):
            scores(2 * t + 1, buf_b, False)
            consume(2 * t, buf_a)
            scores(2 * t + 2, buf_a, False)
            consume(2 * t + 1, buf_b)
            return carry

        lax.fori_loop(0, npairs, pair, 0)

        @pl.when(i % 2 == 1)
        def _():
            scores(i, buf_b, True)
            consume(i - 1, buf_a)
            consume(i, buf_b)

        @pl.when(i % 2 == 0)
        def _():
            scores(i - 1, buf_b, False)
            consume(i - 2, buf_a)
            scores(i, buf_a, True)
            consume(i - 1, buf_b)
            consume(i, buf_a)

    lam = lam_ref[0, 0]
    o = (a1_ref[0:HEAD_W, :] / a1_ref[HEAD_W:HEAD_W + 1, :]
         - lam * (a2_ref[0:HEAD_W, :] / a2_ref[HEAD_W:HEAD_W + 1, :]))
    o = o * lax.rsqrt(jnp.mean(o * o, axis=0, keepdims=True) + LN_EPS) * g_ref[...]
    o_ref[...] = o.T.astype(o_ref.dtype)


def _attention(qt, k, vt, lam, g_col, tq):
    s = k.shape[0]
    nb = s // tq
    score = pltpu.VMEM((tq, tq), F32)
    stat = pltpu.VMEM((1, tq), F32)
    accum = pltpu.VMEM((VT_ROWS, tq), F32)
    return pl.pallas_call(
        functools.partial(_attn_kernel, tq=tq),
        grid=(ATT_HEADS, nb),
        in_specs=[
            pl.BlockSpec(memory_space=pltpu.SMEM),
            pl.BlockSpec((1, HEAD_W, tq), lambda h, i: (i, h, 0)),
            pl.BlockSpec((s, HEAD_W), lambda h, i: (0, h)),
            pl.BlockSpec((nb, 1, VT_ROWS, tq), lambda h, i: (0, h, 0, 0)),
            pl.BlockSpec((HEAD_W, 1), lambda h, i: (0, 0)),
        ],
        out_specs=pl.BlockSpec((tq, HEAD_W), lambda h, i: (i, h)),
        out_shape=jax.ShapeDtypeStruct((s, ATT_WIDTH), BF16),
        scratch_shapes=[score, score, score, score, stat, stat, stat, stat, stat, accum, stat, accum],
        compiler_params=_cparams(("parallel", "arbitrary")),
        name="diff_attention",
    )(lam, qt, k, vt, g_col)


def _pool_kernel(u_ref, halo_ref, w_ref, sc_ref, o_ref, buf_ref, *, tm):
    i = pl.program_id(0)
    halo = halo_ref[...]
    buf_ref[0:POOL_HALO, :] = jnp.where(i > 0, halo, jnp.zeros_like(halo))
    buf_ref[POOL_HALO:, :] = u_ref[...]
    t = i * tm + lax.broadcasted_iota(jnp.int32, (tm, 1), 0)
    for g, w in enumerate(POOL_WINDOWS):
        cols = slice(g * POOL_CH, (g + 1) * POOL_CH)
        acc = buf_ref[POOL_HALO:, cols]
        for back in range(1, w):
            acc = acc + buf_ref[POOL_HALO - back:POOL_HALO - back + tm, cols]
        cnt = jnp.minimum(t + 1, w).astype(F32)
        pooled = acc / cnt - buf_ref[POOL_HALO:, cols]
        mixed = jnp.dot(pooled.astype(BF16), w_ref[g], preferred_element_type=F32)
        o_ref[:, cols] = (mixed * sc_ref[:, cols]).astype(o_ref.dtype)


def _pool(u_pool, pool_w_b, pool_scale, tm):
    s = u_pool.shape[0]
    return pl.pallas_call(
        functools.partial(_pool_kernel, tm=tm),
        grid=(s // tm,),
        in_specs=[
            pl.BlockSpec((tm, POOL_WIDTH), lambda i: (i, 0)),
            pl.BlockSpec((POOL_HALO, POOL_WIDTH),
                         lambda i: (jnp.maximum(i * (tm // POOL_HALO) - 1, 0), 0)),
            pl.BlockSpec((len(POOL_WINDOWS), POOL_CH, POOL_CH), lambda i: (0, 0, 0)),
            pl.BlockSpec((1, POOL_WIDTH), lambda i: (0, 0)),
        ],
        out_specs=pl.BlockSpec((tm, POOL_WIDTH), lambda i: (i, 0)),
        out_shape=jax.ShapeDtypeStruct((s, POOL_WIDTH), BF16),
        scratch_shapes=[pltpu.VMEM((tm + POOL_HALO, POOL_WIDTH), F32)],
        compiler_params=_cparams(("parallel",)),
        name="pool",
    )(u_pool, u_pool, pool_w_b, pool_scale.reshape(1, POOL_WIDTH))


def _ssm_matrices(a_re, a_im, log_dt, b_re, b_im, c_re, c_im, d_skip):
    hp = lax.Precision.HIGHEST
    t_len = SSM_CHUNK
    ar, ai = a_re.astype(F32), a_im.astype(F32)
    dt = jnp.exp(log_dt.astype(F32))[:, None]
    mag = jnp.exp(ar * dt)
    ab_re, ab_im = mag * jnp.cos(ai * dt), mag * jnp.sin(ai * dt)
    den = ar * ar + ai * ai
    nr, ni = ab_re - 1.0, ab_im
    f_re = (nr * ar + ni * ai) / den
    f_im = (ni * ar - nr * ai) / den
    br, bi = b_re.astype(F32), b_im.astype(F32)
    bb_re = f_re[..., None] * br - f_im[..., None] * bi
    bb_im = f_re[..., None] * bi + f_im[..., None] * br
    kk = jnp.arange(t_len + 1, dtype=F32)[:, None, None]
    pmag = jnp.exp(ar * dt * kk)
    pw_re, pw_im = pmag * jnp.cos(ai * dt * kk), pmag * jnp.sin(ai * dt * kk)
    wb_re = pw_re[..., None] * bb_re - pw_im[..., None] * bb_im
    wb_im = pw_re[..., None] * bb_im + pw_im[..., None] * bb_re
    cr, ci = c_re.astype(F32), c_im.astype(F32)
    kmat = (jnp.einsum("kgpi,gop->gkio", wb_re, cr, precision=hp)
            - jnp.einsum("kgpi,gop->gkio", wb_im, ci, precision=hp))
    h = SSM_GROUP_CH

    lag = jnp.arange(t_len)[None, :] - jnp.arange(t_len)[:, None]
    kst = kmat[:, jnp.clip(lag, 0, t_len)] * (lag >= 0)[None, :, :, None, None].astype(F32)
    kc = kst.reshape(SSM_SLABS, SLAB_GROUPS, t_len, t_len, h, h).transpose(0, 2, 1, 4, 3, 5)
    kc = kc.reshape(SSM_SLABS, SSM_ROW, SSM_CROW)

    rev = t_len - 1 - jnp.arange(t_len)
    pst = jnp.stack([wb_re[rev], wb_im[rev]], axis=0)
    pc = pst.reshape(2, t_len, SSM_SLABS, SLAB_GROUPS, SSM_STATE, h).transpose(2, 1, 3, 5, 0, 4)
    pc = pc.reshape(SSM_SLABS, SSM_ROW, 2 * SSM_STATE)

    pr, pi = pw_re[1:], pw_im[1:]
    q_re = pr[:, :, None, :] * cr[None] - pi[:, :, None, :] * ci[None]
    q_im = -pi[:, :, None, :] * cr[None] - pr[:, :, None, :] * ci[None]
    qst = jnp.stack([q_re, q_im], axis=0).reshape(2, t_len, SSM_SLABS, SLAB_GROUPS, h, SSM_STATE)
    qc = qst.transpose(2, 0, 3, 5, 1, 4).reshape(SSM_SLABS, 2 * SLAB_STATE, SSM_CROW)

    a_t = jnp.stack([pw_re[t_len], pw_im[t_len]], axis=0)
    a_t = a_t.reshape(2, SSM_SLABS, SLAB_STATE).transpose(1, 0, 2)
    d_row = jnp.tile(d_skip.astype(F32).reshape(SSM_SLABS, 1, LANES), (1, 1, t_len))

    col = jnp.arange(SSM_ROW)
    rep_k = (jnp.arange(SSM_CROW)[:, None]
             == (col // LANES) * h + col % h).astype(BF16)
    scol = jnp.arange(2 * SLAB_STATE)
    rep_p = (jnp.arange(2 * SSM_STATE)[:, None]
             == (scol // SLAB_STATE) * SSM_STATE + scol % SSM_STATE).astype(BF16)
    return kc.astype(BF16), pc.astype(BF16), qc.astype(BF16), rep_k, rep_p, a_t, d_row


def _gelu_tanh(x):
    return x * (0.5 * (1.0 + jnp.tanh(math.sqrt(2.0 / math.pi) * (x + 0.044715 * (x * x * x)))))


def _lane_group(idx):
    return (idx >> 4) & (SLAB_GROUPS - 1)


def _state_group(idx):
    return (idx >> 6) & (SLAB_GROUPS - 1)


def _expand_block_diag(compact, rep_ref, out_ref, row_group, col_group, step):
    rows = compact.shape[0]
    for c0 in range(0, out_ref.shape[1], step):
        full = jnp.dot(compact, rep_ref[:, c0:c0 + step], preferred_element_type=F32)
        rg = row_group(lax.broadcasted_iota(jnp.int32, (rows, step), 0))
        cg = col_group(lax.broadcasted_iota(jnp.int32, (rows, step), 1) + c0)
        out_ref[:, c0:c0 + step] = jnp.where(rg == cg, full, 0.0).astype(out_ref.dtype)


def _ssm_kernel(u_ref, kc_ref, pc_ref, qc_ref, repk_ref, repp_ref, at_ref, d_ref, o_ref,
                m_ref, p_ref, q_ref, xloc_ref, xprev_ref, st_ref, *, cm):
    @pl.when(pl.program_id(1) == 0)
    def _():
        st_ref[...] = jnp.zeros(st_ref.shape, F32)
        _expand_block_diag(kc_ref[0], repk_ref, m_ref, _lane_group, _lane_group, 512)
        _expand_block_diag(pc_ref[0], repp_ref, p_ref, _lane_group, _state_group, 512)
        _expand_block_diag(qc_ref[0], repk_ref, q_ref, _state_group, _lane_group, 512)

    u = u_ref[0]
    ub = u.astype(BF16)
    xloc_ref[...] = jnp.dot(ub, p_ref[...], preferred_element_type=F32)
    a_r = at_ref[0, 0:1, :]
    a_i = at_ref[0, 1:2, :]

    def step(c, carry):
        xr, xi = carry
        xprev_ref[pl.ds(c, 1), 0:SLAB_STATE] = xr
        xprev_ref[pl.ds(c, 1), SLAB_STATE:] = xi
        br = xloc_ref[pl.ds(c, 1), 0:SLAB_STATE]
        bi = xloc_ref[pl.ds(c, 1), SLAB_STATE:]
        return a_r * xr - a_i * xi + br, a_r * xi + a_i * xr + bi

    xr, xi = lax.fori_loop(0, cm, step, (st_ref[0:1, :], st_ref[1:2, :]))
    st_ref[0:1, :] = xr
    st_ref[1:2, :] = xi

    y = jnp.dot(ub, m_ref[...], preferred_element_type=F32)
    y = y + jnp.dot(xprev_ref[...].astype(BF16), q_ref[...], preferred_element_type=F32)
    y = y + d_ref[0] * u
    o_ref[0] = _gelu_tanh(y)


def _ssm(u_slabs, mats, cm):
    kc, pc, qc, rep_k, rep_p, a_t, d_row = mats
    nslab, s, _ = u_slabs.shape
    nchunk = s // SSM_CHUNK
    u_rows = u_slabs.reshape(nslab, nchunk, SSM_ROW)
    rows = pl.BlockSpec((1, cm, SSM_ROW), lambda j, c: (j, c, 0))

    def per_slab(shape):
        return pl.BlockSpec((1,) + shape, lambda j, c: (j, 0, 0))

    def whole(shape):
        return pl.BlockSpec(shape, lambda j, c: (0, 0))

    y = pl.pallas_call(
        functools.partial(_ssm_kernel, cm=cm),
        grid=(nslab, nchunk // cm),
        in_specs=[rows, per_slab((SSM_ROW, SSM_CROW)), per_slab((SSM_ROW, 2 * SSM_STATE)),
                  per_slab((2 * SLAB_STATE, SSM_CROW)), whole((SSM_CROW, SSM_ROW)),
                  whole((2 * SSM_STATE, 2 * SLAB_STATE)), per_slab((2, SLAB_STATE)), per_slab((1, SSM_ROW))],
        out_specs=rows,
        out_shape=jax.ShapeDtypeStruct((nslab, nchunk, SSM_ROW), F32),
        scratch_shapes=[pltpu.VMEM((SSM_ROW, SSM_ROW), BF16), pltpu.VMEM((SSM_ROW, 2 * SLAB_STATE), BF16),
                        pltpu.VMEM((2 * SLAB_STATE, SSM_ROW), BF16),
                        pltpu.VMEM((cm, 2 * SLAB_STATE), F32), pltpu.VMEM((cm, 2 * SLAB_STATE), F32),
                        pltpu.VMEM((2, SLAB_STATE), F32)],
        compiler_params=_cparams(("parallel", "arbitrary")),
        name="s5_ssm",
    )(u_rows, kc, pc, qc, rep_k, rep_p, a_t, d_row)
    return y.reshape(nslab, s, LANES)


def _merge_kernel(ya_ref, yp_ref, ys_ref, g_ref, h_ref, wa_ref, wp_ref, ws_ref, wg_ref, bg_ref,
                  wo_ref, lg_ref, lb_ref, o_ref, ob_ref, *, alpha):
    ys = jnp.concatenate([ys_ref[c] for c in range(SSM_SLABS)], axis=1)
    z = jnp.dot(ys.astype(BF16), wg_ref[...], preferred_element_type=F32) + bg_ref[...]
    y_ssm = ys * jax.nn.sigmoid(z)
    merged = g_ref[:, 0:D_MODEL] * jnp.dot(ya_ref[...], wa_ref[...], preferred_element_type=F32)
    merged = merged + g_ref[:, D_MODEL:2 * D_MODEL] * jnp.dot(
        yp_ref[...], wp_ref[...], preferred_element_type=F32)
    merged = merged + g_ref[:, 2 * D_MODEL:] * jnp.dot(
        y_ssm.astype(BF16), ws_ref[...], preferred_element_type=F32)
    z = alpha * h_ref[...] + jnp.dot(merged.astype(BF16), wo_ref[...], preferred_element_type=F32)
    y = _ln_rows(z, lg_ref[...], lb_ref[...])
    o_ref[...] = y
    ob_ref[...] = y.astype(BF16)


def _merge(y_attn, y_pool, y_s, gates, h, wa, wp, ws, wg, bg, wo, ln_g, ln_b, alpha, tm):
    s, d = h.shape

    def rows(width):
        return pl.BlockSpec((tm, width), lambda i: (i, 0))

    def whole(shape):
        return pl.BlockSpec(shape, lambda i: (0,) * len(shape), pipeline_mode=pl.Buffered(1))

    return pl.pallas_call(
        functools.partial(_merge_kernel, alpha=alpha),
        grid=(s // tm,),
        in_specs=[rows(ATT_WIDTH), rows(POOL_WIDTH),
                  pl.BlockSpec((SSM_SLABS, tm, LANES), lambda i: (0, i, 0)),
                  rows(GATE_COLS), rows(d),
                  whole((ATT_WIDTH, d)), whole((POOL_WIDTH, d)), whole((SSM_WIDTH, d)),
                  whole((SSM_WIDTH, SSM_WIDTH)), whole((1, SSM_WIDTH)), whole((d, d)),
                  whole((1, d)), whole((1, d))],
        out_specs=[rows(d), rows(d)],
        out_shape=[jax.ShapeDtypeStruct((s, d), F32), jax.ShapeDtypeStruct((s, d), BF16)],
        compiler_params=_cparams(("parallel",)),
        name="merge_out_ln",
    )(y_attn, y_pool, y_s, gates, h, wa, wp, ws, wg, bg.reshape(1, SSM_WIDTH), wo,
      ln_g.reshape(1, d), ln_b.reshape(1, d))


def _mlp_kernel(hb_ref, h_ref, wu_ref, wd_ref, lg_ref, lb_ref, o_ref, ob_ref, *, alpha):
    f = pl.program_id(1)

    @pl.when(f == 0)
    def _():
        o_ref[...] = alpha * h_ref[...]

    up = jnp.dot(hb_ref[...], wu_ref[...], preferred_element_type=F32)
    r = jnp.maximum(up, 0.0)
    o_ref[...] += jnp.dot((r * r).astype(BF16), wd_ref[...], preferred_element_type=F32)

    @pl.when(f == pl.num_programs(1) - 1)
    def _():
        y = _ln_rows(o_ref[...], lg_ref[...], lb_ref[...])
        o_ref[...] = y
        ob_ref[...] = y.astype(BF16)


def _mlp(hb, h, w_up_b, w_down_b, ln_g, ln_b, alpha, tm, tf):
    s, d = h.shape
    rows = pl.BlockSpec((tm, d), lambda i, f: (i, 0))
    vec = pl.BlockSpec((1, d), lambda i, f: (0, 0))
    return pl.pallas_call(
        functools.partial(_mlp_kernel, alpha=alpha),
        grid=(s // tm, D_FF // tf),
        in_specs=[rows,
                  pl.BlockSpec((tm, d), lambda i, f: (i, 0), pipeline_mode=pl.Buffered(1)),
                  pl.BlockSpec((d, tf), lambda i, f: (0, f)),
                  pl.BlockSpec((tf, d), lambda i, f: (f, 0)),
                  vec, vec],
        out_specs=[rows, rows],
        out_shape=[jax.ShapeDtypeStruct((s, d), F32), jax.ShapeDtypeStruct((s, d), BF16)],
        compiler_params=_cparams(("parallel", "arbitrary")),
        name="mlp_ln",
    )(hb, h, w_up_b, w_down_b, ln_g.reshape(1, d), ln_b.reshape(1, d))


def kernel(x, positions, ln_in_g, ln_in_b, w_in, b_gate, lam_q1, lam_k1, lam_q2, lam_k2, subln_g, pool_w, pool_scale, ssm_a_re, ssm_a_im, ssm_log_dt, ssm_b_re, ssm_b_im, ssm_c_re, ssm_c_im, ssm_d, glu_w, glu_b, proj_attn, proj_pool, proj_ssm, w_out, ln1_g, ln1_b, w_up, w_down, ln2_g, ln2_b):
    bsz, seq, d = x.shape
    assert bsz == 1 and d == D_MODEL
    depth = w_in.shape[0]
    alpha = (2.0 * depth) ** 0.25
    tl = _tiles(seq)

    h, hb = _layer_norm(x.reshape(seq, d), ln_in_g, ln_in_b, tl["ln"])
    pos_col = positions.reshape(seq, 1)
    inv_freq = ROPE_THETA ** (-jnp.arange(0, ROT_DIM, 2, dtype=F32) / ROT_DIM)
    invf = jnp.tile(inv_freq, LANES // ROT_HALF).reshape(1, LANES)
    rope = _rope_tables(pos_col, invf, tl["proj"])

    for l in range(depth):
        lam_init = 0.8 - 0.6 * math.exp(-0.3 * l)
        lam = (jnp.exp(jnp.sum(lam_q1[l].astype(F32) * lam_k1[l].astype(F32)))
               - jnp.exp(jnp.sum(lam_q2[l].astype(F32) * lam_k2[l].astype(F32))) + lam_init)
        g_col = (subln_g[l].astype(F32) * (1.0 - lam_init)).reshape(HEAD_W, 1)

        qt, k, vt, u_pool, u_ssm, gates = _in_projections(
            hb, w_in[l].astype(BF16), b_gate[l], rope, tl["proj"], tl["att"])
        y_attn = _attention(qt, k, vt, lam.reshape(1, 1), g_col, tl["att"])
        y_pool = _pool(u_pool, pool_w[l].astype(BF16), pool_scale[l], tl["pool"])
        mats = _ssm_matrices(ssm_a_re[l], ssm_a_im[l], ssm_log_dt[l], ssm_b_re[l], ssm_b_im[l],
                             ssm_c_re[l], ssm_c_im[l], ssm_d[l])
        y_s = _ssm(u_ssm, mats, tl["ssm"])
        h, hb = _merge(y_attn, y_pool, y_s, gates, h,
                       proj_attn[l].astype(BF16), proj_pool[l].astype(BF16), proj_ssm[l].astype(BF16),
                       glu_w[l].astype(BF16), glu_b[l], w_out[l].astype(BF16),
                       ln1_g[l], ln1_b[l], alpha, tl["merge"])
        h, hb = _mlp(hb, h, w_up[l].astype(BF16), w_down[l].astype(BF16), ln2_g[l], ln2_b[l],
                     alpha, tl["mlp"], tl["ff"])
    return h.reshape(bsz, seq, d)
```

```python
import functools
import math

import jax
import jax.numpy as jnp
from jax import lax
from jax.experimental import pallas as pl
from jax.experimental.pallas import tpu as pltpu

F32 = jnp.float32
BF16 = jnp.bfloat16
HIGHEST = lax.Precision.HIGHEST

D_MODEL = 2048
ATT_HEADS = 8
ATT_QK_DIM = 64
HEAD_W = 2 * ATT_QK_DIM
ATT_WIDTH = ATT_HEADS * HEAD_W
ROPE_THETA = 500000.0
ROT_DIM = ATT_QK_DIM // 4
ROT_HALF = ROT_DIM // 2
POOL_WINDOWS = (2, 4, 8, 16)
POOL_CH = 128
POOL_WIDTH = len(POOL_WINDOWS) * POOL_CH
POOL_HALO = 16
SSM_GROUP_CH = 16
SSM_WIDTH = 512
SSM_GROUPS = SSM_WIDTH // SSM_GROUP_CH
SSM_STATE = 64
D_FF = 4 * D_MODEL
LN_EPS = 1e-5
K_OFF = ATT_WIDTH
V_OFF = 2 * ATT_WIDTH
POOL_OFF = 3 * ATT_WIDTH
SSM_OFF = POOL_OFF + POOL_WIDTH
GATE_OFF = SSM_OFF + SSM_WIDTH
GATE_COLS = 3 * D_MODEL

LANES = 128
BF16_ROWS = 16
MXU_COLS = 256
GATE_CHUNK = MXU_COLS
GATE_TILE = 1024
PROJ_TILE = 512
SSM_SLABS = SSM_WIDTH // LANES
SLAB_GROUPS = LANES // SSM_GROUP_CH
SLAB_STATE = SLAB_GROUPS * SSM_STATE
SSM_CHUNK = 16
SSM_ROW = SSM_CHUNK * LANES
VT_ROWS = HEAD_W + BF16_ROWS
VMEM_LIMIT = 56 * 1024 * 1024

NEG_BIG = -1e30
LOG2E = math.log2(math.e)


def _tiles(seq):
    tiles = _tile_table(seq)
    assert all(seq % tiles[name] == 0 for name in ("ln", "proj", "att", "pool", "merge", "mlp"))
    assert (seq // SSM_CHUNK) % tiles["ssm"] == 0 and tiles["proj"] % tiles["att"] == 0
    return tiles


def _tile_table(seq):
    return dict(
        ln=min(512, seq),
        proj=min(1024, seq),
        att=min(1024, seq),
        pool=min(1024, seq),
        ssm=min(256, seq // SSM_CHUNK),
        merge=min(256, seq),
        mlp=min(1024, seq),
        ff=256,
    )


def _cparams(sem):
    return pltpu.CompilerParams(dimension_semantics=sem, vmem_limit_bytes=VMEM_LIMIT)


def _ln_rows(z, g, b):
    mu = jnp.mean(z, axis=-1, keepdims=True)
    zc = z - mu
    var = jnp.mean(zc * zc, axis=-1, keepdims=True)
    return zc * lax.rsqrt(var + LN_EPS) * g + b


def _ln_kernel(x_ref, g_ref, b_ref, h_ref, hb_ref):
    y = _ln_rows(x_ref[...], g_ref[...], b_ref[...])
    h_ref[...] = y
    hb_ref[...] = y.astype(BF16)


def _layer_norm(x2d, g, b, tm):
    s, d = x2d.shape
    row = pl.BlockSpec((tm, d), lambda i: (i, 0))
    vec = pl.BlockSpec((1, d), lambda i: (0, 0))
    return pl.pallas_call(
        _ln_kernel,
        grid=(s // tm,),
        in_specs=[row, vec, vec],
        out_specs=[row, row],
        out_shape=[jax.ShapeDtypeStruct((s, d), F32), jax.ShapeDtypeStruct((s, d), BF16)],
        compiler_params=_cparams(("parallel",)),
        name="ln_in",
    )(x2d, g.reshape(1, d), b.reshape(1, d))


def _rope_kernel(pos_ref, invf_ref, same_ref, up_ref, dn_ref):
    ang = pos_ref[...].astype(F32) * invf_ref[...]
    cos, sin = jnp.cos(ang), jnp.sin(ang)
    d = lax.broadcasted_iota(jnp.int32, ang.shape, 1) % ATT_QK_DIM
    same_ref[...] = jnp.where(d < ROT_DIM, cos, 1.0)
    up_ref[...] = jnp.where(d < ROT_HALF, -sin, 0.0)
    dn_ref[...] = jnp.where((d >= ROT_HALF) & (d < ROT_DIM), sin, 0.0)


def _rope_tables(pos_col, invf, tm):
    s = pos_col.shape[0]
    tab = pl.BlockSpec((tm, LANES), lambda i: (i, 0))
    return pl.pallas_call(
        _rope_kernel,
        grid=(s // tm,),
        in_specs=[pl.BlockSpec((tm, 1), lambda i: (i, 0)), pl.BlockSpec((1, LANES), lambda i: (0, 0))],
        out_specs=[tab, tab, tab],
        out_shape=[jax.ShapeDtypeStruct((s, LANES), F32)] * 3,
        compiler_params=_cparams(("parallel",)),
        name="rope_tables",
    )(pos_col, invf)


def _rotate(xg, same_ref, up_ref, dn_ref):
    up = pltpu.roll(xg, LANES - ROT_HALF, axis=1)
    dn = pltpu.roll(xg, ROT_HALF, axis=1)
    return xg * same_ref[...] + up * up_ref[...] + dn * dn_ref[...]


def _xw(x_ref, w_ref):
    return jnp.dot(x_ref[...], w_ref[...].astype(BF16), preferred_element_type=F32)


def _proj_q_kernel(x_ref, w_ref, same_ref, up_ref, dn_ref, o_ref, *, tq):
    acc = _xw(x_ref, w_ref)
    scale = LOG2E / math.sqrt(ATT_QK_DIM)
    for c in range(acc.shape[1] // LANES):
        rot = _rotate(acc[:, c * LANES:(c + 1) * LANES], same_ref, up_ref, dn_ref) * scale
        for b in range(o_ref.shape[0]):
            o_ref[b, c * LANES:(c + 1) * LANES, :] = rot[b * tq:(b + 1) * tq, :].T.astype(o_ref.dtype)


def _proj_k_kernel(x_ref, w_ref, same_ref, up_ref, dn_ref, o_ref):
    acc = _xw(x_ref, w_ref)
    for c in range(acc.shape[1] // LANES):
        cols = slice(c * LANES, (c + 1) * LANES)
        o_ref[:, cols] = _rotate(acc[:, cols], same_ref, up_ref, dn_ref).astype(o_ref.dtype)


def _proj_v_kernel(x_ref, w_ref, o_ref, *, tk):
    acc = _xw(x_ref, w_ref)
    ones = jnp.ones((BF16_ROWS, tk), o_ref.dtype)
    for b in range(o_ref.shape[0]):
        for hh in range(o_ref.shape[1]):
            blk = acc[b * tk:(b + 1) * tk, hh * HEAD_W:(hh + 1) * HEAD_W]
            o_ref[b, hh, 0:HEAD_W, :] = blk.T.astype(o_ref.dtype)
            o_ref[b, hh, HEAD_W:, :] = ones


def _proj_plain_kernel(x_ref, w_ref, o_ref):
    o_ref[...] = _xw(x_ref, w_ref).astype(o_ref.dtype)


def _proj_slab_kernel(x_ref, w_ref, o_ref):
    acc = _xw(x_ref, w_ref)
    for c in range(o_ref.shape[0]):
        o_ref[c] = acc[:, c * LANES:(c + 1) * LANES]


def _proj_gate_kernel(x_ref, w_ref, b_ref, o_ref):
    x = x_ref[...]
    for c in range(o_ref.shape[1] // GATE_CHUNK):
        cols = slice(c * GATE_CHUNK, (c + 1) * GATE_CHUNK)
        z = jnp.dot(x, w_ref[:, cols].astype(BF16), preferred_element_type=F32) + b_ref[:, cols]
        o_ref[:, cols] = (0.5 * jnp.tanh(0.5 * z) + 0.5).astype(o_ref.dtype)


def _in_projections(hb, w_in, b_gate, layer, rope, tm, tq):
    s, d = hb.shape
    tn = PROJ_TILE
    nb = s // tq
    x_spec = pl.BlockSpec((tm, d), lambda i, j: (i, 0))

    def w_spec(off, width=tn):
        return pl.BlockSpec((None, d, width), lambda i, j: (layer, 0, j + off // width))

    out_tile = pl.BlockSpec((tm, tn), lambda i, j: (i, j))
    rope_specs = [pl.BlockSpec((tm, LANES), lambda i, j: (i, 0))] * 3
    sem = _cparams(("parallel", "arbitrary"))

    qt = pl.pallas_call(
        functools.partial(_proj_q_kernel, tq=tq),
        grid=(s // tm, ATT_WIDTH // tn),
        in_specs=[x_spec, w_spec(0)] + rope_specs,
        out_specs=pl.BlockSpec((tm // tq, tn, tq), lambda i, j: (i, j, 0)),
        out_shape=jax.ShapeDtypeStruct((nb, ATT_WIDTH, tq), BF16),
        compiler_params=sem, name="proj_q",
    )(hb, w_in, *rope)

    k = pl.pallas_call(
        _proj_k_kernel,
        grid=(s // tm, ATT_WIDTH // tn),
        in_specs=[x_spec, w_spec(K_OFF)] + rope_specs,
        out_specs=out_tile,
        out_shape=jax.ShapeDtypeStruct((s, ATT_WIDTH), BF16),
        compiler_params=sem, name="proj_k",
    )(hb, w_in, *rope)

    vt = pl.pallas_call(
        functools.partial(_proj_v_kernel, tk=tq),
        grid=(s // tm, ATT_WIDTH // tn),
        in_specs=[x_spec, w_spec(V_OFF)],
        out_specs=pl.BlockSpec((tm // tq, tn // HEAD_W, VT_ROWS, tq), lambda i, j: (i, j, 0, 0)),
        out_shape=jax.ShapeDtypeStruct((nb, ATT_HEADS, VT_ROWS, tq), BF16),
        compiler_params=sem, name="proj_v",
    )(hb, w_in)

    u_pool = pl.pallas_call(
        _proj_plain_kernel,
        grid=(s // tm, POOL_WIDTH // tn),
        in_specs=[x_spec, w_spec(POOL_OFF)],
        out_specs=out_tile,
        out_shape=jax.ShapeDtypeStruct((s, POOL_WIDTH), F32),
        compiler_params=sem, name="proj_pool_in",
    )(hb, w_in)

    u_ssm = pl.pallas_call(
        _proj_slab_kernel,
        grid=(s // tm, SSM_WIDTH // tn),
        in_specs=[x_spec, w_spec(SSM_OFF)],
        out_specs=pl.BlockSpec((SSM_SLABS, tm, LANES), lambda i, j: (0, i, 0)),
        out_shape=jax.ShapeDtypeStruct((SSM_SLABS, s, LANES), F32),
        compiler_params=sem, name="proj_ssm_in",
    )(hb, w_in)

    gates = pl.pallas_call(
        _proj_gate_kernel,
        grid=(s // tm, GATE_COLS // GATE_TILE),
        in_specs=[x_spec, w_spec(GATE_OFF, GATE_TILE),
                  pl.BlockSpec((None, 1, GATE_TILE), lambda i, j: (layer, 0, j))],
        out_specs=pl.BlockSpec((tm, GATE_TILE), lambda i, j: (i, j)),
        out_shape=jax.ShapeDtypeStruct((s, GATE_COLS), BF16),
        compiler_params=sem, name="proj_gates",
    )(hb, w_in, b_gate.reshape(b_gate.shape[0], 1, GATE_COLS))
    return qt, k, vt, u_pool, u_ssm, gates


def _attn_kernel(lam_ref, qt_ref, k_ref, vt_ref, g_ref, o_ref,
                 sa1, sa2, sb1, sb2, ba1, ba2, bb1, bb2, m1_ref, a1_ref, m2_ref, a2_ref, *, tq):
    i = pl.program_id(1)
    qt = qt_ref[0]
    comp = lax.broadcasted_iota(jnp.int32, qt.shape, 0) < ATT_QK_DIM
    zero = jnp.zeros_like(qt)
    q1t = jnp.where(comp, qt, zero)
    q2t = jnp.where(comp, zero, qt)

    for m_ref, a_ref in ((m1_ref, a1_ref), (m2_ref, a2_ref)):
        m_ref[...] = jnp.full(m_ref.shape, NEG_BIG, F32)
        a_ref[...] = jnp.zeros(a_ref.shape, F32)

    buf_a = ((sa1, ba1), (sa2, ba2))
    buf_b = ((sb1, bb1), (sb2, bb2))

    def scores(j, buf, masked):
        kb = k_ref[pl.ds(pl.multiple_of(j * tq, tq), tq), :]
        for qct, (s_ref, bm_ref) in zip((q1t, q2t), buf):
            s = jnp.dot(kb, qct, preferred_element_type=F32)
            if masked:
                keep = (lax.broadcasted_iota(jnp.int32, s.shape, 0)
                        <= lax.broadcasted_iota(jnp.int32, s.shape, 1))
                s = jnp.where(keep, s, NEG_BIG)
            s_ref[...] = s
            bm_ref[...] = jnp.max(s, axis=0, keepdims=True)

    def consume(j, buf):
        vb = vt_ref[j, 0]
        for (s_ref, bm_ref), m_ref, a_ref in zip(buf, (m1_ref, m2_ref), (a1_ref, a2_ref)):
            m_old = m_ref[...]
            m_new = jnp.maximum(m_old, bm_ref[...])
            alpha = jnp.exp2(m_old - m_new)
            p = jnp.exp2(s_ref[...] - m_new).astype(BF16)
            a_ref[...] = alpha * a_ref[...] + jnp.dot(vb, p, preferred_element_type=F32)
            m_ref[...] = m_new

    @pl.when(i == 0)
    def _():
        scores(0, buf_a, True)
        consume(0, buf_a)

    @pl.when(i > 0)
    def _():
        scores(0, buf_a, False)
        npairs = lax.shift_right_logical(i - 1, 1)

        def pair(t, carry):
            scores(2 * t + 1, buf_b, False)
            consume(2 * t, buf_a)
            scores(2 * t + 2, buf_a, False)
            consume(2 * t + 1, buf_b)
            return carry

        lax.fori_loop(0, npairs, pair, 0)

        @pl.when(i % 2 == 1)
        def _():
            scores(i, buf_b, True)
            consume(i - 1, buf_a)
            consume(i, buf_b)

        @pl.when(i % 2 == 0)
        def _():
            scores(i - 1, buf_b, False)
            consume(i - 2, buf_a)
            scores(i, buf_a, True)
            consume(i - 1, buf_b)
            consume(i, buf_a)

    lam = lam_ref[0, 0]
    o = (a1_ref[0:HEAD_W, :] / a1_ref[HEAD_W:HEAD_W + 1, :]
         - lam * (a2_ref[0:HEAD_W, :] / a2_ref[HEAD_W:HEAD_W + 1, :]))
    o = o * lax.rsqrt(jnp.mean(o * o, axis=0, keepdims=True) + LN_EPS) * g_ref[...]
    o_ref[...] = o.T.astype(o_ref.dtype)


def _attention(qt, k, vt, lam, g_col, tq):
    s = k.shape[0]
    nb = s // tq
    score = pltpu.VMEM((tq, tq), F32)
    stat = pltpu.VMEM((1, tq), F32)
    accum = pltpu.VMEM((VT_ROWS, tq), F32)
    return pl.pallas_call(
        functools.partial(_attn_kernel, tq=tq),
        grid=(ATT_HEADS, nb),
        in_specs=[
            pl.BlockSpec(memory_space=pltpu.SMEM),
            pl.BlockSpec((1, HEAD_W, tq), lambda h, i: (i, h, 0)),
            pl.BlockSpec((s, HEAD_W), lambda h, i: (0, h)),
            pl.BlockSpec((nb, 1, VT_ROWS, tq), lambda h, i: (0, h, 0, 0)),
            pl.BlockSpec((HEAD_W, 1), lambda h, i: (0, 0)),
        ],
        out_specs=pl.BlockSpec((tq, HEAD_W), lambda h, i: (i, h)),
        out_shape=jax.ShapeDtypeStruct((s, ATT_WIDTH), BF16),
        scratch_shapes=[score, score, score, score, stat, stat, stat, stat, stat, accum, stat, accum],
        compiler_params=_cparams(("parallel", "arbitrary")),
        name="diff_attention",
    )(lam, qt, k, vt, g_col)


def _pool_kernel(u_ref, halo_ref, w_ref, sc_ref, o_ref, buf_ref, *, tm):
    i = pl.program_id(0)
    halo = halo_ref[...]
    buf_ref[0:POOL_HALO, :] = jnp.where(i > 0, halo, jnp.zeros_like(halo))
    buf_ref[POOL_HALO:, :] = u_ref[...]
    t = i * tm + lax.broadcasted_iota(jnp.int32, (tm, 1), 0)
    for g, w in enumerate(POOL_WINDOWS):
        cols = slice(g * POOL_CH, (g + 1) * POOL_CH)
        acc = buf_ref[POOL_HALO:, cols]
        for back in range(1, w):
            acc = acc + buf_ref[POOL_HALO - back:POOL_HALO - back + tm, cols]
        cnt = jnp.minimum(t + 1, w).astype(F32)
        pooled = acc / cnt - buf_ref[POOL_HALO:, cols]
        mixed = jnp.dot(pooled.astype(BF16), w_ref[g].astype(BF16), preferred_element_type=F32)
        o_ref[:, cols] = (mixed * sc_ref[:, cols]).astype(o_ref.dtype)


def _pool(u_pool, pool_w, pool_scale, layer, tm):
    s = u_pool.shape[0]
    return pl.pallas_call(
        functools.partial(_pool_kernel, tm=tm),
        grid=(s // tm,),
        in_specs=[
            pl.BlockSpec((tm, POOL_WIDTH), lambda i: (i, 0)),
            pl.BlockSpec((POOL_HALO, POOL_WIDTH),
                         lambda i: (jnp.maximum(i * (tm // POOL_HALO) - 1, 0), 0)),
            pl.BlockSpec((None, len(POOL_WINDOWS), POOL_CH, POOL_CH), lambda i: (layer, 0, 0, 0)),
            pl.BlockSpec((None, 1, POOL_WIDTH), lambda i: (layer, 0, 0)),
        ],
        out_specs=pl.BlockSpec((tm, POOL_WIDTH), lambda i: (i, 0)),
        out_shape=jax.ShapeDtypeStruct((s, POOL_WIDTH), BF16),
        scratch_shapes=[pltpu.VMEM((tm + POOL_HALO, POOL_WIDTH), F32)],
        compiler_params=_cparams(("parallel",)),
        name="pool",
    )(u_pool, u_pool, pool_w, pool_scale.reshape(pool_scale.shape[0], 1, POOL_WIDTH))


def _ssm_params(a_re, a_im, log_dt, b_re, b_im, c_re, c_im, d_skip):
    depth = a_re.shape[0]
    f = lambda x: x.astype(F32)
    ldt = jnp.broadcast_to(f(log_dt)[:, :, None], a_re.shape)
    row = lambda x: f(x).reshape(depth, SSM_SLABS, 1, SLAB_STATE)
    col = lambda x: f(x).reshape(depth, SSM_SLABS, SLAB_STATE, 1)
    bt = lambda x: (f(x).transpose(0, 3, 1, 2)
                    .reshape(depth, SSM_GROUP_CH, SSM_SLABS, SLAB_STATE).transpose(0, 2, 1, 3))
    cp = lambda x: (f(x).transpose(0, 3, 1, 2)
                    .reshape(depth, SSM_STATE, SSM_SLABS, LANES).transpose(0, 2, 1, 3))
    return (row(a_re), row(a_im), row(ldt), col(a_re), col(a_im), col(ldt),
            bt(b_re), bt(b_im), cp(c_re), cp(c_im), f(d_skip).reshape(depth, SSM_SLABS, 1, LANES))


def _gelu_tanh(x):
    return x * (0.5 * (1.0 + jnp.tanh(math.sqrt(2.0 / math.pi) * (x + 0.044715 * (x * x * x)))))


def _ssm_build(ar_ref, ai_ref, ldt_ref, arc_ref, aic_ref, ldtc_ref, btr_ref, bti_ref, cpr_ref, cpi_ref,
               m_ref, p_ref, q_ref, at_ref):
    t_len = SSM_CHUNK
    ar, ai = ar_ref[...], ai_ref[...]
    dt = jnp.exp(ldt_ref[...])
    mag = jnp.exp(ar * dt)
    ab_re, ab_im = mag * jnp.cos(ai * dt), mag * jnp.sin(ai * dt)
    den = ar * ar + ai * ai
    nr, ni = ab_re - 1.0, ab_im
    f_re = (nr * ar + ni * ai) / den
    f_im = (ni * ar - nr * ai) / den
    btr, bti = btr_ref[...], bti_ref[...]
    bb_re = f_re * btr - f_im * bti
    bb_im = f_re * bti + f_im * btr

    def tile_rows(x):
        return jnp.concatenate([x] * SLAB_GROUPS, axis=0)

    own_b = ((lax.broadcasted_iota(jnp.int32, (LANES, SLAB_STATE), 0) // SSM_GROUP_CH)
             == (lax.broadcasted_iota(jnp.int32, (LANES, SLAB_STATE), 1) // SSM_STATE))
    bbd_re = jnp.where(own_b, tile_rows(bb_re), 0.0)
    bbd_im = jnp.where(own_b, tile_rows(bb_im), 0.0)
    own_c = ((lax.broadcasted_iota(jnp.int32, (SLAB_STATE, LANES), 0) // SSM_STATE)
             == (lax.broadcasted_iota(jnp.int32, (SLAB_STATE, LANES), 1) // SSM_GROUP_CH))
    cbd_re = jnp.where(own_c, tile_rows(cpr_ref[...]), 0.0)
    cbd_im = jnp.where(own_c, tile_rows(cpi_ref[...]), 0.0)

    zero_blk = jnp.zeros((LANES, LANES), m_ref.dtype)
    for s in range(t_len):
        for t in range(s):
            m_ref[s * LANES:(s + 1) * LANES, t * LANES:(t + 1) * LANES] = zero_blk

    for k in range(t_len + 1):
        pm = jnp.exp(ar * dt * float(k))
        pr, pi = pm * jnp.cos(ai * dt * float(k)), pm * jnp.sin(ai * dt * float(k))
        if k == t_len:
            at_ref[0:1, :] = pr
            at_ref[1:2, :] = pi
            break
        wb_re = pr * bbd_re - pi * bbd_im
        wb_im = pr * bbd_im + pi * bbd_re
        lag_blk = (jnp.dot(wb_re, cbd_re, precision=HIGHEST, preferred_element_type=F32)
                   - jnp.dot(wb_im, cbd_im, precision=HIGHEST, preferred_element_type=F32))
        lag_blk = lag_blk.astype(m_ref.dtype)
        for s in range(t_len - k):
            m_ref[s * LANES:(s + 1) * LANES, (s + k) * LANES:(s + k + 1) * LANES] = lag_blk
        rows = slice((t_len - 1 - k) * LANES, (t_len - k) * LANES)
        p_ref[rows, 0:SLAB_STATE] = wb_re.astype(p_ref.dtype)
        p_ref[rows, SLAB_STATE:] = wb_im.astype(p_ref.dtype)

    arc, aic = arc_ref[...], aic_ref[...]
    dtc = jnp.exp(ldtc_ref[...])
    kk = lax.broadcasted_iota(jnp.int32, (SLAB_STATE, LANES), 1).astype(F32)
    pmc = jnp.exp(arc * dtc * kk)
    prc, pic = pmc * jnp.cos(aic * dtc * kk), pmc * jnp.sin(aic * dtc * kk)
    for t in range(t_len):
        pr_col, pi_col = prc[:, t + 1:t + 2], pic[:, t + 1:t + 2]
        cols = slice(t * LANES, (t + 1) * LANES)
        q_ref[0:SLAB_STATE, cols] = (pr_col * cbd_re - pi_col * cbd_im).astype(q_ref.dtype)
        q_ref[SLAB_STATE:, cols] = (-pi_col * cbd_re - pr_col * cbd_im).astype(q_ref.dtype)


def _ssm_kernel(u_ref, ar_ref, ai_ref, ldt_ref, arc_ref, aic_ref, ldtc_ref, btr_ref, bti_ref,
                cpr_ref, cpi_ref, d_ref, o_ref,
                m_ref, p_ref, q_ref, at_ref, urow_ref, xloc_ref, xprev_ref, st_ref, *, cm):
    t_len = SSM_CHUNK

    @pl.when(pl.program_id(1) == 0)
    def _():
        st_ref[...] = jnp.zeros(st_ref.shape, F32)
        _ssm_build(ar_ref, ai_ref, ldt_ref, arc_ref, aic_ref, ldtc_ref, btr_ref, bti_ref,
                   cpr_ref, cpi_ref, m_ref, p_ref, q_ref, at_ref)

    for t in range(t_len):
        urow_ref[:, t * LANES:(t + 1) * LANES] = u_ref[pl.ds(t, cm, stride=t_len), :]
    u = urow_ref[...]
    ub = u.astype(BF16)
    xloc_ref[...] = jnp.dot(ub, p_ref[...], preferred_element_type=F32)
    a_r = at_ref[0:1, :]
    a_i = at_ref[1:2, :]

    def step(c, carry):
        xr, xi = carry
        xprev_ref[pl.ds(c, 1), 0:SLAB_STATE] = xr
        xprev_ref[pl.ds(c, 1), SLAB_STATE:] = xi
        br = xloc_ref[pl.ds(c, 1), 0:SLAB_STATE]
        bi = xloc_ref[pl.ds(c, 1), SLAB_STATE:]
        return a_r * xr - a_i * xi + br, a_r * xi + a_i * xr + bi

    xr, xi = lax.fori_loop(0, cm, step, (st_ref[0:1, :], st_ref[1:2, :]))
    st_ref[0:1, :] = xr
    st_ref[1:2, :] = xi

    y = jnp.dot(ub, m_ref[...], preferred_element_type=F32)
    y = y + jnp.dot(xprev_ref[...].astype(BF16), q_ref[...], preferred_element_type=F32)
    d_row = jnp.concatenate([d_ref[...]] * t_len, axis=1)
    y = _gelu_tanh(y + d_row * u)
    for t in range(t_len):
        o_ref[pl.ds(t, cm, stride=t_len), :] = y[:, t * LANES:(t + 1) * LANES]


def _ssm(u_slabs, params, layer, cm):
    nslab, s, _ = u_slabs.shape
    rows = pl.BlockSpec((None, cm * SSM_CHUNK, LANES), lambda j, c: (j, c, 0))

    def per_slab(shape):
        return pl.BlockSpec((None, None) + shape, lambda j, c: (layer, j, 0, 0))

    lane_vec = per_slab((1, SLAB_STATE))
    col_vec = per_slab((SLAB_STATE, 1))
    bt_spec = per_slab((SSM_GROUP_CH, SLAB_STATE))
    cp_spec = per_slab((SSM_STATE, LANES))
    return pl.pallas_call(
        functools.partial(_ssm_kernel, cm=cm),
        grid=(nslab, s // (cm * SSM_CHUNK)),
        in_specs=[rows, lane_vec, lane_vec, lane_vec, col_vec, col_vec, col_vec,
                  bt_spec, bt_spec, cp_spec, cp_spec, per_slab((1, LANES))],
        out_specs=rows,
        out_shape=jax.ShapeDtypeStruct((nslab, s, LANES), F32),
        scratch_shapes=[pltpu.VMEM((SSM_ROW, SSM_ROW), BF16), pltpu.VMEM((SSM_ROW, 2 * SLAB_STATE), BF16),
                        pltpu.VMEM((2 * SLAB_STATE, SSM_ROW), BF16), pltpu.VMEM((2, SLAB_STATE), F32),
                        pltpu.VMEM((cm, SSM_ROW), F32),
                        pltpu.VMEM((cm, 2 * SLAB_STATE), F32), pltpu.VMEM((cm, 2 * SLAB_STATE), F32),
                        pltpu.VMEM((2, SLAB_STATE), F32)],
        compiler_params=_cparams(("parallel", "arbitrary")),
        name="s5_ssm",
    )(u_slabs, *params)


def _merge_kernel(ya_ref, yp_ref, ys_ref, g_ref, h_ref, wa_ref, wp_ref, ws_ref, wg_ref, bg_ref,
                  wo_ref, lg_ref, lb_ref, o_ref, ob_ref, *, alpha):
    ys = jnp.concatenate([ys_ref[c] for c in range(SSM_SLABS)], axis=1)
    z = jnp.dot(ys.astype(BF16), wg_ref[...], preferred_element_type=F32) + bg_ref[...]
    y_ssm = ys * jax.nn.sigmoid(z)
    merged = g_ref[:, 0:D_MODEL] * jnp.dot(ya_ref[...], wa_ref[...], preferred_element_type=F32)
    merged = merged + g_ref[:, D_MODEL:2 * D_MODEL] * jnp.dot(
        yp_ref[...], wp_ref[...], preferred_element_type=F32)
    merged = merged + g_ref[:, 2 * D_MODEL:] * jnp.dot(
        y_ssm.astype(BF16), ws_ref[...], preferred_element_type=F32)
    z = alpha * h_ref[...] + jnp.dot(merged.astype(BF16), wo_ref[...], preferred_element_type=F32)
    y = _ln_rows(z, lg_ref[...], lb_ref[...])
    o_ref[...] = y
    ob_ref[...] = y.astype(BF16)


def _merge(y_attn, y_pool, y_s, gates, h, wa, wp, ws, wg, bg, wo, ln_g, ln_b, layer, alpha, tm):
    s, d = h.shape

    def rows(width):
        return pl.BlockSpec((tm, width), lambda i: (i, 0))

    def whole(rows_, cols_):
        return pl.BlockSpec((None, rows_, cols_), lambda i: (layer, 0, 0), pipeline_mode=pl.Buffered(1))

    vec = lambda x: x.reshape(x.shape[0], 1, x.shape[1])
    return pl.pallas_call(
        functools.partial(_merge_kernel, alpha=alpha),
        grid=(s // tm,),
        in_specs=[rows(ATT_WIDTH), rows(POOL_WIDTH),
                  pl.BlockSpec((SSM_SLABS, tm, LANES), lambda i: (0, i, 0)),
                  rows(GATE_COLS), rows(d),
                  whole(ATT_WIDTH, d), whole(POOL_WIDTH, d), whole(SSM_WIDTH, d),
                  whole(SSM_WIDTH, SSM_WIDTH), whole(1, SSM_WIDTH), whole(d, d),
                  whole(1, d), whole(1, d)],
        out_specs=[rows(d), rows(d)],
        out_shape=[jax.ShapeDtypeStruct((s, d), F32), jax.ShapeDtypeStruct((s, d), BF16)],
        compiler_params=_cparams(("parallel",)),
        name="merge_out_ln",
    )(y_attn, y_pool, y_s, gates, h, wa, wp, ws, wg, vec(bg), wo, vec(ln_g), vec(ln_b))


def _mlp_kernel(hb_ref, h_ref, wu_ref, wd_ref, lg_ref, lb_ref, o_ref, ob_ref, *, alpha):
    f = pl.program_id(1)

    @pl.when(f == 0)
    def _():
        o_ref[...] = alpha * h_ref[...]

    up = jnp.dot(hb_ref[...], wu_ref[...].astype(BF16), preferred_element_type=F32)
    r = jnp.maximum(up, 0.0)
    o_ref[...] += jnp.dot((r * r).astype(BF16), wd_ref[...].astype(BF16), preferred_element_type=F32)

    @pl.when(f == pl.num_programs(1) - 1)
    def _():
        y = _ln_rows(o_ref[...], lg_ref[...], lb_ref[...])
        o_ref[...] = y
        ob_ref[...] = y.astype(BF16)


def _mlp(hb, h, w_up, w_down, ln_g, ln_b, layer, alpha, tm, tf):
    s, d = h.shape
    rows = pl.BlockSpec((tm, d), lambda i, f: (i, 0))
    vec = pl.BlockSpec((None, 1, d), lambda i, f: (layer, 0, 0))
    vec3 = lambda x: x.reshape(x.shape[0], 1, x.shape[1])
    return pl.pallas_call(
        functools.partial(_mlp_kernel, alpha=alpha),
        grid=(s // tm, D_FF // tf),
        in_specs=[rows,
                  pl.BlockSpec((tm, d), lambda i, f: (i, 0), pipeline_mode=pl.Buffered(1)),
                  pl.BlockSpec((None, d, tf), lambda i, f: (layer, 0, f)),
                  pl.BlockSpec((None, tf, d), lambda i, f: (layer, f, 0)),
                  vec, vec],
        out_specs=[rows, rows],
        out_shape=[jax.ShapeDtypeStruct((s, d), F32), jax.ShapeDtypeStruct((s, d), BF16)],
        compiler_params=_cparams(("parallel", "arbitrary")),
        name="mlp_ln",
    )(hb, h, w_up, w_down, vec3(ln_g), vec3(ln_b))


def kernel(x, positions, ln_in_g, ln_in_b, w_in, b_gate, lam_q1, lam_k1, lam_q2, lam_k2, subln_g, pool_w, pool_scale, ssm_a_re, ssm_a_im, ssm_log_dt, ssm_b_re, ssm_b_im, ssm_c_re, ssm_c_im, ssm_d, glu_w, glu_b, proj_attn, proj_pool, proj_ssm, w_out, ln1_g, ln1_b, w_up, w_down, ln2_g, ln2_b):
    bsz, seq, d = x.shape
    assert bsz == 1 and d == D_MODEL
    depth = w_in.shape[0]
    alpha = (2.0 * depth) ** 0.25
    tl = _tiles(seq)

    h, hb = _layer_norm(x.reshape(seq, d), ln_in_g, ln_in_b, tl["ln"])
    pos_col = positions.reshape(seq, 1)
    inv_freq = ROPE_THETA ** (-jnp.arange(0, ROT_DIM, 2, dtype=F32) / ROT_DIM)
    invf = jnp.tile(inv_freq, LANES // ROT_HALF).reshape(1, LANES)
    rope = _rope_tables(pos_col, invf, tl["proj"])
    ssm_params = _ssm_params(ssm_a_re, ssm_a_im, ssm_log_dt, ssm_b_re, ssm_b_im, ssm_c_re, ssm_c_im, ssm_d)
    merge_w = [w.astype(BF16) for w in (proj_attn, proj_pool, proj_ssm, glu_w)]
    w_out_b = w_out.astype(BF16)

    for l in range(depth):
        lam_init = 0.8 - 0.6 * math.exp(-0.3 * l)
        lam = (jnp.exp(jnp.sum(lam_q1[l].astype(F32) * lam_k1[l].astype(F32)))
               - jnp.exp(jnp.sum(lam_q2[l].astype(F32) * lam_k2[l].astype(F32))) + lam_init)
        g_col = (subln_g[l].astype(F32) * (1.0 - lam_init)).reshape(HEAD_W, 1)

        qt, k, vt, u_pool, u_ssm, gates = _in_projections(hb, w_in, b_gate, l, rope, tl["proj"], tl["att"])
        y_attn = _attention(qt, k, vt, lam.reshape(1, 1), g_col, tl["att"])
        y_pool = _pool(u_pool, pool_w, pool_scale, l, tl["pool"])
        y_s = _ssm(u_ssm, ssm_params, l, tl["ssm"])
        h, hb = _merge(y_attn, y_pool, y_s, gates, h, *merge_w, glu_b, w_out_b, ln1_g, ln1_b,
                       l, alpha, tl["merge"])
        h, hb = _mlp(hb, h, w_up, w_down, ln2_g, ln2_b, l, alpha, tl["mlp"], tl["ff"])
    return h.reshape(bsz, seq, d)
```

```python
import functools
import math

import jax
import jax.numpy as jnp
from jax import lax
from jax.experimental import pallas as pl
from jax.experimental.pallas import tpu as pltpu

F32 = jnp.float32
BF16 = jnp.bfloat16
HIGHEST = lax.Precision.HIGHEST

D_MODEL = 2048
ATT_HEADS = 8
ATT_QK_DIM = 64
HEAD_W = 2 * ATT_QK_DIM
ATT_WIDTH = ATT_HEADS * HEAD_W
ROPE_THETA = 500000.0
ROT_DIM = ATT_QK_DIM // 4
ROT_HALF = ROT_DIM // 2
POOL_WINDOWS = (2, 4, 8, 16)
POOL_CH = 128
POOL_WIDTH = len(POOL_WINDOWS) * POOL_CH
POOL_HALO = 16
SSM_GROUP_CH = 16
SSM_WIDTH = 512
SSM_GROUPS = SSM_WIDTH // SSM_GROUP_CH
SSM_STATE = 64
D_FF = 4 * D_MODEL
LN_EPS = 1e-5
K_OFF = ATT_WIDTH
V_OFF = 2 * ATT_WIDTH
POOL_OFF = 3 * ATT_WIDTH
SSM_OFF = POOL_OFF + POOL_WIDTH
GATE_OFF = SSM_OFF + SSM_WIDTH
GATE_COLS = 3 * D_MODEL

LANES = 128
BF16_ROWS = 16
MXU_COLS = 256
GATE_CHUNK = MXU_COLS
GATE_TILE = 1024
PROJ_TILE = 512
SSM_SLABS = SSM_WIDTH // LANES
SLAB_GROUPS = LANES // SSM_GROUP_CH
SLAB_STATE = SLAB_GROUPS * SSM_STATE
SSM_CHUNK = 16
SSM_ROW = SSM_CHUNK * LANES
VT_ROWS = HEAD_W + BF16_ROWS
VMEM_LIMIT = 56 * 1024 * 1024

NEG_BIG = -1e30
LOG2E = math.log2(math.e)


def _tiles(seq):
    tiles = _tile_table(seq)
    assert all(seq % tiles[name] == 0 for name in ("ln", "proj", "att", "pool", "merge", "mlp"))
    assert (seq // SSM_CHUNK) % tiles["ssm"] == 0 and tiles["proj"] % tiles["att"] == 0
    return tiles


def _tile_table(seq):
    return dict(
        ln=min(512, seq),
        proj=min(1024, seq),
        att=min(1024, seq),
        pool=min(1024, seq),
        ssm=min(256, seq // SSM_CHUNK),
        merge=min(256, seq),
        mlp=min(1024, seq),
        ff=512,
    )


def _cparams(sem):
    return pltpu.CompilerParams(dimension_semantics=sem, vmem_limit_bytes=VMEM_LIMIT)


def _ln_rows(z, g, b):
    mu = jnp.mean(z, axis=-1, keepdims=True)
    zc = z - mu
    var = jnp.mean(zc * zc, axis=-1, keepdims=True)
    return zc * lax.rsqrt(var + LN_EPS) * g + b


def _ln_kernel(x_ref, g_ref, b_ref, h_ref, hb_ref):
    y = _ln_rows(x_ref[...], g_ref[...], b_ref[...])
    h_ref[...] = y
    hb_ref[...] = y.astype(BF16)


def _layer_norm(x2d, g, b, tm):
    s, d = x2d.shape
    row = pl.BlockSpec((tm, d), lambda i: (i, 0))
    vec = pl.BlockSpec((1, d), lambda i: (0, 0))
    return pl.pallas_call(
        _ln_kernel,
        grid=(s // tm,),
        in_specs=[row, vec, vec],
        out_specs=[row, row],
        out_shape=[jax.ShapeDtypeStruct((s, d), F32), jax.ShapeDtypeStruct((s, d), BF16)],
        compiler_params=_cparams(("parallel",)),
        name="ln_in",
    )(x2d, g.reshape(1, d), b.reshape(1, d))


def _rope_kernel(pos_ref, invf_ref, same_ref, up_ref, dn_ref):
    ang = pos_ref[...].astype(F32) * invf_ref[...]
    cos, sin = jnp.cos(ang), jnp.sin(ang)
    d = lax.broadcasted_iota(jnp.int32, ang.shape, 1) % ATT_QK_DIM
    same_ref[...] = jnp.where(d < ROT_DIM, cos, 1.0)
    up_ref[...] = jnp.where(d < ROT_HALF, -sin, 0.0)
    dn_ref[...] = jnp.where((d >= ROT_HALF) & (d < ROT_DIM), sin, 0.0)


def _rope_tables(pos_col, invf, tm):
    s = pos_col.shape[0]
    tab = pl.BlockSpec((tm, LANES), lambda i: (i, 0))
    return pl.pallas_call(
        _rope_kernel,
        grid=(s // tm,),
        in_specs=[pl.BlockSpec((tm, 1), lambda i: (i, 0)), pl.BlockSpec((1, LANES), lambda i: (0, 0))],
        out_specs=[tab, tab, tab],
        out_shape=[jax.ShapeDtypeStruct((s, LANES), F32)] * 3,
        compiler_params=_cparams(("parallel",)),
        name="rope_tables",
    )(pos_col, invf)


def _rotate(xg, same_ref, up_ref, dn_ref):
    up = pltpu.roll(xg, LANES - ROT_HALF, axis=1)
    dn = pltpu.roll(xg, ROT_HALF, axis=1)
    return xg * same_ref[...] + up * up_ref[...] + dn * dn_ref[...]


def _xw(x_ref, w_ref):
    return jnp.dot(x_ref[...], w_ref[...].astype(BF16), preferred_element_type=F32)


def _proj_q_kernel(x_ref, w_ref, same_ref, up_ref, dn_ref, o_ref, *, tq):
    acc = _xw(x_ref, w_ref)
    scale = LOG2E / math.sqrt(ATT_QK_DIM)
    for c in range(acc.shape[1] // LANES):
        rot = _rotate(acc[:, c * LANES:(c + 1) * LANES], same_ref, up_ref, dn_ref) * scale
        for b in range(o_ref.shape[0]):
            o_ref[b, c * LANES:(c + 1) * LANES, :] = rot[b * tq:(b + 1) * tq, :].T.astype(o_ref.dtype)


def _proj_k_kernel(x_ref, w_ref, same_ref, up_ref, dn_ref, o_ref):
    acc = _xw(x_ref, w_ref)
    for c in range(acc.shape[1] // LANES):
        cols = slice(c * LANES, (c + 1) * LANES)
        o_ref[:, cols] = _rotate(acc[:, cols], same_ref, up_ref, dn_ref).astype(o_ref.dtype)


def _proj_v_kernel(x_ref, w_ref, o_ref, *, tk):
    acc = _xw(x_ref, w_ref)
    ones = jnp.ones((BF16_ROWS, tk), o_ref.dtype)
    for b in range(o_ref.shape[0]):
        for hh in range(o_ref.shape[1]):
            blk = acc[b * tk:(b + 1) * tk, hh * HEAD_W:(hh + 1) * HEAD_W]
            o_ref[b, hh, 0:HEAD_W, :] = blk.T.astype(o_ref.dtype)
            o_ref[b, hh, HEAD_W:, :] = ones


def _proj_plain_kernel(x_ref, w_ref, o_ref):
    o_ref[...] = _xw(x_ref, w_ref).astype(o_ref.dtype)


def _proj_slab_kernel(x_ref, w_ref, o_ref):
    acc = _xw(x_ref, w_ref)
    for c in range(o_ref.shape[0]):
        o_ref[c] = acc[:, c * LANES:(c + 1) * LANES]


def _proj_gate_kernel(x_ref, w_ref, b_ref, o_ref):
    x = x_ref[...]
    for c in range(o_ref.shape[1] // GATE_CHUNK):
        cols = slice(c * GATE_CHUNK, (c + 1) * GATE_CHUNK)
        z = jnp.dot(x, w_ref[:, cols].astype(BF16), preferred_element_type=F32) + b_ref[:, cols]
        o_ref[:, cols] = (0.5 * jnp.tanh(0.5 * z) + 0.5).astype(o_ref.dtype)


def _in_projections(hb, w_in, b_gate, layer, rope, tm, tq):
    s, d = hb.shape
    tn = PROJ_TILE
    nb = s // tq
    x_spec = pl.BlockSpec((tm, d), lambda i, j: (i, 0))

    def w_spec(off, width=tn):
        return pl.BlockSpec((None, d, width), lambda i, j: (layer, 0, j + off // width))

    out_tile = pl.BlockSpec((tm, tn), lambda i, j: (i, j))
    rope_specs = [pl.BlockSpec((tm, LANES), lambda i, j: (i, 0))] * 3
    sem = _cparams(("parallel", "arbitrary"))

    qt = pl.pallas_call(
        functools.partial(_proj_q_kernel, tq=tq),
        grid=(s // tm, ATT_WIDTH // tn),
        in_specs=[x_spec, w_spec(0)] + rope_specs,
        out_specs=pl.BlockSpec((tm // tq, tn, tq), lambda i, j: (i, j, 0)),
        out_shape=jax.ShapeDtypeStruct((nb, ATT_WIDTH, tq), BF16),
        compiler_params=sem, name="proj_q",
    )(hb, w_in, *rope)

    k = pl.pallas_call(
        _proj_k_kernel,
        grid=(s // tm, ATT_WIDTH // tn),
        in_specs=[x_spec, w_spec(K_OFF)] + rope_specs,
        out_specs=out_tile,
        out_shape=jax.ShapeDtypeStruct((s, ATT_WIDTH), BF16),
        compiler_params=sem, name="proj_k",
    )(hb, w_in, *rope)

    vt = pl.pallas_call(
        functools.partial(_proj_v_kernel, tk=tq),
        grid=(s // tm, ATT_WIDTH // tn),
        in_specs=[x_spec, w_spec(V_OFF)],
        out_specs=pl.BlockSpec((tm // tq, tn // HEAD_W, VT_ROWS, tq), lambda i, j: (i, j, 0, 0)),
        out_shape=jax.ShapeDtypeStruct((nb, ATT_HEADS, VT_ROWS, tq), BF16),
        compiler_params=sem, name="proj_v",
    )(hb, w_in)

    u_pool = pl.pallas_call(
        _proj_plain_kernel,
        grid=(s // tm, POOL_WIDTH // tn),
        in_specs=[x_spec, w_spec(POOL_OFF)],
        out_specs=out_tile,
        out_shape=jax.ShapeDtypeStruct((s, POOL_WIDTH), F32),
        compiler_params=sem, name="proj_pool_in",
    )(hb, w_in)

    u_ssm = pl.pallas_call(
        _proj_slab_kernel,
        grid=(s // tm, SSM_WIDTH // tn),
        in_specs=[x_spec, w_spec(SSM_OFF)],
        out_specs=pl.BlockSpec((SSM_SLABS, tm, LANES), lambda i, j: (0, i, 0)),
        out_shape=jax.ShapeDtypeStruct((SSM_SLABS, s, LANES), F32),
        compiler_params=sem, name="proj_ssm_in",
    )(hb, w_in)

    gates = pl.pallas_call(
        _proj_gate_kernel,
        grid=(s // tm, GATE_COLS // GATE_TILE),
        in_specs=[x_spec, w_spec(GATE_OFF, GATE_TILE),
                  pl.BlockSpec((None, 1, GATE_TILE), lambda i, j: (layer, 0, j))],
        out_specs=pl.BlockSpec((tm, GATE_TILE), lambda i, j: (i, j)),
        out_shape=jax.ShapeDtypeStruct((s, GATE_COLS), BF16),
        compiler_params=sem, name="proj_gates",
    )(hb, w_in, b_gate.reshape(b_gate.shape[0], 1, GATE_COLS))
    return qt, k, vt, u_pool, u_ssm, gates


def _attn_kernel(lam_ref, qt_ref, k_ref, vt_ref, g_ref, o_ref,
                 sa1, sa2, sb1, sb2, ba1, ba2, bb1, bb2, m1_ref, a1_ref, m2_ref, a2_ref, *, tq):
    i = pl.program_id(1)
    qt = qt_ref[0]
    comp = lax.broadcasted_iota(jnp.int32, qt.shape, 0) < ATT_QK_DIM
    zero = jnp.zeros_like(qt)
    q1t = jnp.where(comp, qt, zero)
    q2t = jnp.where(comp, zero, qt)

    for m_ref, a_ref in ((m1_ref, a1_ref), (m2_ref, a2_ref)):
        m_ref[...] = jnp.full(m_ref.shape, NEG_BIG, F32)
        a_ref[...] = jnp.zeros(a_ref.shape, F32)

    buf_a = ((sa1, ba1), (sa2, ba2))
    buf_b = ((sb1, bb1), (sb2, bb2))

    def scores(j, buf, masked):
        kb = k_ref[pl.ds(pl.multiple_of(j * tq, tq), tq), :]
        for qct, (s_ref, bm_ref) in zip((q1t, q2t), buf):
            s = jnp.dot(kb, qct, preferred_element_type=F32)
            if masked:
                keep = (lax.broadcasted_iota(jnp.int32, s.shape, 0)
                        <= lax.broadcasted_iota(jnp.int32, s.shape, 1))
                s = jnp.where(keep, s, NEG_BIG)
            s_ref[...] = s
            bm_ref[...] = jnp.max(s, axis=0, keepdims=True)

    def consume(j, buf):
        vb = vt_ref[j, 0]
        for (s_ref, bm_ref), m_ref, a_ref in zip(buf, (m1_ref, m2_ref), (a1_ref, a2_ref)):
            m_old = m_ref[...]
            m_new = jnp.maximum(m_old, bm_ref[...])
            alpha = jnp.exp2(m_old - m_new)
            p = jnp.exp2(s_ref[...] - m_new).astype(BF16)
            a_ref[...] = alpha * a_ref[...] + jnp.dot(vb, p, preferred_element_type=F32)
            m_ref[...] = m_new

    @pl.when(i == 0)
    def _():
        scores(0, buf_a, True)
        consume(0, buf_a)

    @pl.when(i > 0)
    def _():
        scores(0, buf_a, False)
        npairs = lax.shift_right_logical(i - 1, 1)

        def pair(t, carry):
            scores(2 * t + 1, buf_b, False)
            consume(2 * t, buf_a)
            scores(2 * t + 2, buf_a, False)
            consume(2 * t + 1, buf_b)
            return carry

        lax.fori_loop(0, npairs, pair, 0)

        @pl.when(i % 2 == 1)
        def _():
            scores(i, buf_b, True)
            consume(i - 1, buf_a)
            consume(i, buf_b)

        @pl.when(i % 2 == 0)
        def _():
            scores(i - 1, buf_b, False)
            consume(i - 2, buf_a)
            scores(i, buf_a, True)
            consume(i - 1, buf_b)
            consume(i, buf_a)

    lam = lam_ref[0, 0]
    o = (a1_ref[0:HEAD_W, :] / a1_ref[HEAD_W:HEAD_W + 1, :]
         - lam * (a2_ref[0:HEAD_W, :] / a2_ref[HEAD_W:HEAD_W + 1, :]))
    o = o * lax.rsqrt(jnp.mean(o * o, axis=0, keepdims=True) + LN_EPS) * g_ref[...]
    o_ref[...] = o.T.astype(o_ref.dtype)


def _attention(qt, k, vt, lam, g_col, tq):
    s = k.shape[0]
    nb = s // tq
    score = pltpu.VMEM((tq, tq), F32)
    stat = pltpu.VMEM((1, tq), F32)
    accum = pltpu.VMEM((VT_ROWS, tq), F32)
    return pl.pallas_call(
        functools.partial(_attn_kernel, tq=tq),
        grid=(ATT_HEADS, nb),
        in_specs=[
            pl.BlockSpec(memory_space=pltpu.SMEM),
            pl.BlockSpec((1, HEAD_W, tq), lambda h, i: (i, h, 0)),
            pl.BlockSpec((s, HEAD_W), lambda h, i: (0, h)),
            pl.BlockSpec((nb, 1, VT_ROWS, tq), lambda h, i: (0, h, 0, 0)),
            pl.BlockSpec((HEAD_W, 1), lambda h, i: (0, 0)),
        ],
        out_specs=pl.BlockSpec((tq, HEAD_W), lambda h, i: (i, h)),
        out_shape=jax.ShapeDtypeStruct((s, ATT_WIDTH), BF16),
        scratch_shapes=[score, score, score, score, stat, stat, stat, stat, stat, accum, stat, accum],
        compiler_params=_cparams(("parallel", "arbitrary")),
        name="diff_attention",
    )(lam, qt, k, vt, g_col)


def _pool_kernel(u_ref, halo_ref, w_ref, sc_ref, o_ref, buf_ref, *, tm):
    i = pl.program_id(0)
    halo = halo_ref[...]
    buf_ref[0:POOL_HALO, :] = jnp.where(i > 0, halo, jnp.zeros_like(halo))
    buf_ref[POOL_HALO:, :] = u_ref[...]
    t = i * tm + lax.broadcasted_iota(jnp.int32, (tm, 1), 0)
    for g, w in enumerate(POOL_WINDOWS):
        cols = slice(g * POOL_CH, (g + 1) * POOL_CH)
        acc = buf_ref[POOL_HALO:, cols]
        for back in range(1, w):
            acc = acc + buf_ref[POOL_HALO - back:POOL_HALO - back + tm, cols]
        cnt = jnp.minimum(t + 1, w).astype(F32)
        pooled = acc / cnt - buf_ref[POOL_HALO:, cols]
        mixed = jnp.dot(pooled.astype(BF16), w_ref[g].astype(BF16), preferred_element_type=F32)
        o_ref[:, cols] = (mixed * sc_ref[:, cols]).astype(o_ref.dtype)


def _pool(u_pool, pool_w, pool_scale, layer, tm):
    s = u_pool.shape[0]
    return pl.pallas_call(
        functools.partial(_pool_kernel, tm=tm),
        grid=(s // tm,),
        in_specs=[
            pl.BlockSpec((tm, POOL_WIDTH), lambda i: (i, 0)),
            pl.BlockSpec((POOL_HALO, POOL_WIDTH),
                         lambda i: (jnp.maximum(i * (tm // POOL_HALO) - 1, 0), 0)),
            pl.BlockSpec((None, len(POOL_WINDOWS), POOL_CH, POOL_CH), lambda i: (layer, 0, 0, 0)),
            pl.BlockSpec((None, 1, POOL_WIDTH), lambda i: (layer, 0, 0)),
        ],
        out_specs=pl.BlockSpec((tm, POOL_WIDTH), lambda i: (i, 0)),
        out_shape=jax.ShapeDtypeStruct((s, POOL_WIDTH), BF16),
        scratch_shapes=[pltpu.VMEM((tm + POOL_HALO, POOL_WIDTH), F32)],
        compiler_params=_cparams(("parallel",)),
        name="pool",
    )(u_pool, u_pool, pool_w, pool_scale.reshape(pool_scale.shape[0], 1, POOL_WIDTH))


def _ssm_params(a_re, a_im, log_dt, b_re, b_im, c_re, c_im, d_skip):
    depth = a_re.shape[0]
    f = lambda x: x.astype(F32)
    ldt = jnp.broadcast_to(f(log_dt)[:, :, None], a_re.shape)
    row = lambda x: f(x).reshape(depth, SSM_SLABS, 1, SLAB_STATE)
    col = lambda x: f(x).reshape(depth, SSM_SLABS, SLAB_STATE, 1)
    bt = lambda x: (f(x).transpose(0, 3, 1, 2)
                    .reshape(depth, SSM_GROUP_CH, SSM_SLABS, SLAB_STATE).transpose(0, 2, 1, 3))
    cp = lambda x: (f(x).transpose(0, 3, 1, 2)
                    .reshape(depth, SSM_STATE, SSM_SLABS, LANES).transpose(0, 2, 1, 3))
    return (row(a_re), row(a_im), row(ldt), col(a_re), col(a_im), col(ldt),
            bt(b_re), bt(b_im), cp(c_re), cp(c_im), f(d_skip).reshape(depth, SSM_SLABS, 1, LANES))


def _gelu_tanh(x):
    return x * (0.5 * (1.0 + jnp.tanh(math.sqrt(2.0 / math.pi) * (x + 0.044715 * (x * x * x)))))


def _ssm_build(ar_ref, ai_ref, ldt_ref, arc_ref, aic_ref, ldtc_ref, btr_ref, bti_ref, cpr_ref, cpi_ref,
               m_ref, p_ref, q_ref, at_ref):
    t_len = SSM_CHUNK
    ar, ai = ar_ref[...], ai_ref[...]
    dt = jnp.exp(ldt_ref[...])
    mag = jnp.exp(ar * dt)
    ab_re, ab_im = mag * jnp.cos(ai * dt), mag * jnp.sin(ai * dt)
    den = ar * ar + ai * ai
    nr, ni = ab_re - 1.0, ab_im
    f_re = (nr * ar + ni * ai) / den
    f_im = (ni * ar - nr * ai) / den
    btr, bti = btr_ref[...], bti_ref[...]
    bb_re = f_re * btr - f_im * bti
    bb_im = f_re * bti + f_im * btr

    def tile_rows(x):
        return jnp.concatenate([x] * SLAB_GROUPS, axis=0)

    own_b = ((lax.broadcasted_iota(jnp.int32, (LANES, SLAB_STATE), 0) // SSM_GROUP_CH)
             == (lax.broadcasted_iota(jnp.int32, (LANES, SLAB_STATE), 1) // SSM_STATE))
    bbd_re = jnp.where(own_b, tile_rows(bb_re), 0.0)
    bbd_im = jnp.where(own_b, tile_rows(bb_im), 0.0)
    own_c = ((lax.broadcasted_iota(jnp.int32, (SLAB_STATE, LANES), 0) // SSM_STATE)
             == (lax.broadcasted_iota(jnp.int32, (SLAB_STATE, LANES), 1) // SSM_GROUP_CH))
    cbd_re = jnp.where(own_c, tile_rows(cpr_ref[...]), 0.0)
    cbd_im = jnp.where(own_c, tile_rows(cpi_ref[...]), 0.0)

    zero_blk = jnp.zeros((LANES, LANES), m_ref.dtype)
    for s in range(t_len):
        for t in range(s):
            m_ref[s * LANES:(s + 1) * LANES, t * LANES:(t + 1) * LANES] = zero_blk

    for k in range(t_len + 1):
        pm = jnp.exp(ar * dt * float(k))
        pr, pi = pm * jnp.cos(ai * dt * float(k)), pm * jnp.sin(ai * dt * float(k))
        if k == t_len:
            at_ref[0:1, :] = pr
            at_ref[1:2, :] = pi
            break
        wb_re = pr * bbd_re - pi * bbd_im
        wb_im = pr * bbd_im + pi * bbd_re
        lag_blk = (jnp.dot(wb_re, cbd_re, precision=HIGHEST, preferred_element_type=F32)
                   - jnp.dot(wb_im, cbd_im, precision=HIGHEST, preferred_element_type=F32))
        lag_blk = lag_blk.astype(m_ref.dtype)
        for s in range(t_len - k):
            m_ref[s * LANES:(s + 1) * LANES, (s + k) * LANES:(s + k + 1) * LANES] = lag_blk
        rows = slice((t_len - 1 - k) * LANES, (t_len - k) * LANES)
        p_ref[rows, 0:SLAB_STATE] = wb_re.astype(p_ref.dtype)
        p_ref[rows, SLAB_STATE:] = wb_im.astype(p_ref.dtype)

    arc, aic = arc_ref[...], aic_ref[...]
    dtc = jnp.exp(ldtc_ref[...])
    kk = lax.broadcasted_iota(jnp.int32, (SLAB_STATE, LANES), 1).astype(F32)
    pmc = jnp.exp(arc * dtc * kk)
    prc, pic = pmc * jnp.cos(aic * dtc * kk), pmc * jnp.sin(aic * dtc * kk)
    for t in range(t_len):
        pr_col, pi_col = prc[:, t + 1:t + 2], pic[:, t + 1:t + 2]
        cols = slice(t * LANES, (t + 1) * LANES)
        q_ref[0:SLAB_STATE, cols] = (pr_col * cbd_re - pi_col * cbd_im).astype(q_ref.dtype)
        q_ref[SLAB_STATE:, cols] = (-pi_col * cbd_re - pr_col * cbd_im).astype(q_ref.dtype)


def _ssm_kernel(u_ref, ar_ref, ai_ref, ldt_ref, arc_ref, aic_ref, ldtc_ref, btr_ref, bti_ref,
                cpr_ref, cpi_ref, d_ref, o_ref,
                m_ref, p_ref, q_ref, at_ref, urow_ref, xloc_ref, xprev_ref, st_ref, *, cm):
    t_len = SSM_CHUNK

    @pl.when(pl.program_id(1) == 0)
    def _():
        st_ref[...] = jnp.zeros(st_ref.shape, F32)
        _ssm_build(ar_ref, ai_ref, ldt_ref, arc_ref, aic_ref, ldtc_ref, btr_ref, bti_ref,
                   cpr_ref, cpi_ref, m_ref, p_ref, q_ref, at_ref)

    for t in range(t_len):
        urow_ref[:, t * LANES:(t + 1) * LANES] = u_ref[pl.ds(t, cm, stride=t_len), :]
    u = urow_ref[...]
    ub = u.astype(BF16)
    xloc_ref[...] = jnp.dot(ub, p_ref[...], preferred_element_type=F32)
    a_r = at_ref[0:1, :]
    a_i = at_ref[1:2, :]

    def step(c, carry):
        xr, xi = carry
        xprev_ref[pl.ds(c, 1), 0:SLAB_STATE] = xr
        xprev_ref[pl.ds(c, 1), SLAB_STATE:] = xi
        br = xloc_ref[pl.ds(c, 1), 0:SLAB_STATE]
        bi = xloc_ref[pl.ds(c, 1), SLAB_STATE:]
        return a_r * xr - a_i * xi + br, a_r * xi + a_i * xr + bi

    xr, xi = lax.fori_loop(0, cm, step, (st_ref[0:1, :], st_ref[1:2, :]))
    st_ref[0:1, :] = xr
    st_ref[1:2, :] = xi

    y = jnp.dot(ub, m_ref[...], preferred_element_type=F32)
    y = y + jnp.dot(xprev_ref[...].astype(BF16), q_ref[...], preferred_element_type=F32)
    d_row = jnp.concatenate([d_ref[...]] * t_len, axis=1)
    y = _gelu_tanh(y + d_row * u)
    for t in range(t_len):
        o_ref[pl.ds(t, cm, stride=t_len), :] = y[:, t * LANES:(t + 1) * LANES]


def _ssm(u_slabs, params, layer, cm):
    nslab, s, _ = u_slabs.shape
    rows = pl.BlockSpec((None, cm * SSM_CHUNK, LANES), lambda j, c: (j, c, 0))

    def per_slab(shape):
        return pl.BlockSpec((None, None) + shape, lambda j, c: (layer, j, 0, 0))

    lane_vec = per_slab((1, SLAB_STATE))
    col_vec = per_slab((SLAB_STATE, 1))
    bt_spec = per_slab((SSM_GROUP_CH, SLAB_STATE))
    cp_spec = per_slab((SSM_STATE, LANES))
    return pl.pallas_call(
        functools.partial(_ssm_kernel, cm=cm),
        grid=(nslab, s // (cm * SSM_CHUNK)),
        in_specs=[rows, lane_vec, lane_vec, lane_vec, col_vec, col_vec, col_vec,
                  bt_spec, bt_spec, cp_spec, cp_spec, per_slab((1, LANES))],
        out_specs=rows,
        out_shape=jax.ShapeDtypeStruct((nslab, s, LANES), F32),
        scratch_shapes=[pltpu.VMEM((SSM_ROW, SSM_ROW), BF16), pltpu.VMEM((SSM_ROW, 2 * SLAB_STATE), BF16),
                        pltpu.VMEM((2 * SLAB_STATE, SSM_ROW), BF16), pltpu.VMEM((2, SLAB_STATE), F32),
                        pltpu.VMEM((cm, SSM_ROW), F32),
                        pltpu.VMEM((cm, 2 * SLAB_STATE), F32), pltpu.VMEM((cm, 2 * SLAB_STATE), F32),
                        pltpu.VMEM((2, SLAB_STATE), F32)],
        compiler_params=_cparams(("parallel", "arbitrary")),
        name="s5_ssm",
    )(u_slabs, *params)


def _merge_kernel(ya_ref, yp_ref, ys_ref, g_ref, h_ref, wa_ref, wp_ref, ws_ref, wg_ref, bg_ref,
                  wo_ref, lg_ref, lb_ref, o_ref, ob_ref, *, alpha):
    ys = jnp.concatenate([ys_ref[c] for c in range(SSM_SLABS)], axis=1)
    z = jnp.dot(ys.astype(BF16), wg_ref[...], preferred_element_type=F32) + bg_ref[...]
    y_ssm = ys * jax.nn.sigmoid(z)
    merged = g_ref[:, 0:D_MODEL] * jnp.dot(ya_ref[...], wa_ref[...], preferred_element_type=F32)
    merged = merged + g_ref[:, D_MODEL:2 * D_MODEL] * jnp.dot(
        yp_ref[...], wp_ref[...], preferred_element_type=F32)
    merged = merged + g_ref[:, 2 * D_MODEL:] * jnp.dot(
        y_ssm.astype(BF16), ws_ref[...], preferred_element_type=F32)
    z = alpha * h_ref[...] + jnp.dot(merged.astype(BF16), wo_ref[...], preferred_element_type=F32)
    y = _ln_rows(z, lg_ref[...], lb_ref[...])
    o_ref[...] = y
    ob_ref[...] = y.astype(BF16)


def _merge(y_attn, y_pool, y_s, gates, h, wa, wp, ws, wg, bg, wo, ln_g, ln_b, layer, alpha, tm):
    s, d = h.shape

    def rows(width):
        return pl.BlockSpec((tm, width), lambda i: (i, 0))

    def whole(rows_, cols_):
        return pl.BlockSpec((None, rows_, cols_), lambda i: (layer, 0, 0), pipeline_mode=pl.Buffered(1))

    vec = lambda x: x.reshape(x.shape[0], 1, x.shape[1])
    return pl.pallas_call(
        functools.partial(_merge_kernel, alpha=alpha),
        grid=(s // tm,),
        in_specs=[rows(ATT_WIDTH), rows(POOL_WIDTH),
                  pl.BlockSpec((SSM_SLABS, tm, LANES), lambda i: (0, i, 0)),
                  rows(GATE_COLS), rows(d),
                  whole(ATT_WIDTH, d), whole(POOL_WIDTH, d), whole(SSM_WIDTH, d),
                  whole(SSM_WIDTH, SSM_WIDTH), whole(1, SSM_WIDTH), whole(d, d),
                  whole(1, d), whole(1, d)],
        out_specs=[rows(d), rows(d)],
        out_shape=[jax.ShapeDtypeStruct((s, d), F32), jax.ShapeDtypeStruct((s, d), BF16)],
        compiler_params=_cparams(("parallel",)),
        name="merge_out_ln",
    )(y_attn, y_pool, y_s, gates, h, wa, wp, ws, wg, vec(bg), wo, vec(ln_g), vec(ln_b))


def _mlp_kernel(hb_ref, h_ref, wu_ref, wd_ref, lg_ref, lb_ref, o_ref, ob_ref, *, alpha):
    f = pl.program_id(1)

    @pl.when(f == 0)
    def _():
        o_ref[...] = alpha * h_ref[...]

    up = jnp.dot(hb_ref[...], wu_ref[...], preferred_element_type=F32)
    r = jnp.maximum(up, 0.0)
    o_ref[...] += jnp.dot((r * r).astype(BF16), wd_ref[...], preferred_element_type=F32)

    @pl.when(f == pl.num_programs(1) - 1)
    def _():
        y = _ln_rows(o_ref[...], lg_ref[...], lb_ref[...])
        o_ref[...] = y
        ob_ref[...] = y.astype(BF16)


def _mlp(hb, h, w_up, w_down, ln_g, ln_b, layer, alpha, tm, tf):
    s, d = h.shape
    rows = pl.BlockSpec((tm, d), lambda i, f: (i, 0))
    vec = pl.BlockSpec((None, 1, d), lambda i, f: (layer, 0, 0))
    vec3 = lambda x: x.reshape(x.shape[0], 1, x.shape[1])
    return pl.pallas_call(
        functools.partial(_mlp_kernel, alpha=alpha),
        grid=(s // tm, D_FF // tf),
        in_specs=[rows,
                  pl.BlockSpec((tm, d), lambda i, f: (i, 0), pipeline_mode=pl.Buffered(1)),
                  pl.BlockSpec((None, d, tf), lambda i, f: (layer, 0, f)),
                  pl.BlockSpec((None, tf, d), lambda i, f: (layer, f, 0)),
                  vec, vec],
        out_specs=[rows, rows],
        out_shape=[jax.ShapeDtypeStruct((s, d), F32), jax.ShapeDtypeStruct((s, d), BF16)],
        compiler_params=_cparams(("parallel", "arbitrary")),
        name="mlp_ln",
    )(hb, h, w_up, w_down, vec3(ln_g), vec3(ln_b))


def kernel(x, positions, ln_in_g, ln_in_b, w_in, b_gate, lam_q1, lam_k1, lam_q2, lam_k2, subln_g, pool_w, pool_scale, ssm_a_re, ssm_a_im, ssm_log_dt, ssm_b_re, ssm_b_im, ssm_c_re, ssm_c_im, ssm_d, glu_w, glu_b, proj_attn, proj_pool, proj_ssm, w_out, ln1_g, ln1_b, w_up, w_down, ln2_g, ln2_b):
    bsz, seq, d = x.shape
    assert bsz == 1 and d == D_MODEL
    depth = w_in.shape[0]
    alpha = (2.0 * depth) ** 0.25
    tl = _tiles(seq)

    h, hb = _layer_norm(x.reshape(seq, d), ln_in_g, ln_in_b, tl["ln"])
    pos_col = positions.reshape(seq, 1)
    inv_freq = ROPE_THETA ** (-jnp.arange(0, ROT_DIM, 2, dtype=F32) / ROT_DIM)
    invf = jnp.tile(inv_freq, LANES // ROT_HALF).reshape(1, LANES)
    rope = _rope_tables(pos_col, invf, tl["proj"])
    ssm_params = _ssm_params(ssm_a_re, ssm_a_im, ssm_log_dt, ssm_b_re, ssm_b_im, ssm_c_re, ssm_c_im, ssm_d)
    merge_w = [w.astype(BF16) for w in (proj_attn, proj_pool, proj_ssm, glu_w)]
    w_out_b = w_out.astype(BF16)
    w_up_b, w_down_b = w_up.astype(BF16), w_down.astype(BF16)

    for l in range(depth):
        lam_init = 0.8 - 0.6 * math.exp(-0.3 * l)
        lam = (jnp.exp(jnp.sum(lam_q1[l].astype(F32) * lam_k1[l].astype(F32)))
               - jnp.exp(jnp.sum(lam_q2[l].astype(F32) * lam_k2[l].astype(F32))) + lam_init)
        g_col = (subln_g[l].astype(F32) * (1.0 - lam_init)).reshape(HEAD_W, 1)

        qt, k, vt, u_pool, u_ssm, gates = _in_projections(hb, w_in, b_gate, l, rope, tl["proj"], tl["att"])
        y_attn = _attention(qt, k, vt, lam.reshape(1, 1), g_col, tl["att"])
        y_pool = _pool(u_pool, pool_w, pool_scale, l, tl["pool"])
        y_s = _ssm(u_ssm, ssm_params, l, tl["ssm"])
        h, hb = _merge(y_attn, y_pool, y_s, gates, h, *merge_w, glu_b, w_out_b, ln1_g, ln1_b,
                       l, alpha, tl["merge"])
        h, hb = _mlp(hb, h, w_up_b, w_down_b, ln2_g, ln2_b, l, alpha, tl["mlp"], tl["ff"])
    return h.reshape(bsz, seq, d)
```

```python
import functools
import math

import jax
import jax.numpy as jnp
from jax import lax
from jax.experimental import pallas as pl
from jax.experimental.pallas import tpu as pltpu

F32 = jnp.float32
BF16 = jnp.bfloat16
HIGHEST = lax.Precision.HIGHEST

D_MODEL = 2048
ATT_HEADS = 8
ATT_QK_DIM = 64
HEAD_W = 2 * ATT_QK_DIM
ATT_WIDTH = ATT_HEADS * HEAD_W
ROPE_THETA = 500000.0
ROT_DIM = ATT_QK_DIM // 4
ROT_HALF = ROT_DIM // 2
POOL_WINDOWS = (2, 4, 8, 16)
POOL_CH = 128
POOL_WIDTH = len(POOL_WINDOWS) * POOL_CH
POOL_HALO = 16
SSM_GROUP_CH = 16
SSM_WIDTH = 512
SSM_GROUPS = SSM_WIDTH // SSM_GROUP_CH
SSM_STATE = 64
D_FF = 4 * D_MODEL
LN_EPS = 1e-5
K_OFF = ATT_WIDTH
V_OFF = 2 * ATT_WIDTH
POOL_OFF = 3 * ATT_WIDTH
SSM_OFF = POOL_OFF + POOL_WIDTH
GATE_OFF = SSM_OFF + SSM_WIDTH
GATE_COLS = 3 * D_MODEL

LANES = 128
BF16_ROWS = 16
MXU_COLS = 256
GATE_CHUNK = MXU_COLS
GATE_TILE = 1024
PROJ_TILE = 512
SSM_SLABS = SSM_WIDTH // LANES
SLAB_GROUPS = LANES // SSM_GROUP_CH
SLAB_STATE = SLAB_GROUPS * SSM_STATE
SSM_CHUNK = 16
SSM_ROW = SSM_CHUNK * LANES
VT_ROWS = HEAD_W + BF16_ROWS
VMEM_LIMIT = 56 * 1024 * 1024

NEG_BIG = -1e30
LOG2E = math.log2(math.e)


def _tiles(seq):
    tiles = _tile_table(seq)
    assert all(seq % tiles[name] == 0 for name in ("ln", "proj", "att", "pool", "merge", "mlp"))
    assert (seq // SSM_CHUNK) % tiles["ssm"] == 0 and tiles["proj"] % tiles["att"] == 0
    return tiles


def _tile_table(seq):
    return dict(
        ln=min(512, seq),
        proj=min(1024, seq),
        att=min(1024, seq),
        pool=min(1024, seq),
        ssm=min(256, seq // SSM_CHUNK),
        merge=min(256, seq),
        mlp=min(1024, seq),
        ff=512,
    )


def _cparams(sem):
    return pltpu.CompilerParams(dimension_semantics=sem, vmem_limit_bytes=VMEM_LIMIT)


def _ln_rows(z, g, b):
    mu = jnp.mean(z, axis=-1, keepdims=True)
    zc = z - mu
    var = jnp.mean(zc * zc, axis=-1, keepdims=True)
    return zc * lax.rsqrt(var + LN_EPS) * g + b


def _ln_kernel(x_ref, g_ref, b_ref, h_ref, hb_ref):
    y = _ln_rows(x_ref[...], g_ref[...], b_ref[...])
    h_ref[...] = y
    hb_ref[...] = y.astype(BF16)


def _layer_norm(x2d, g, b, tm):
    s, d = x2d.shape
    row = pl.BlockSpec((tm, d), lambda i: (i, 0))
    vec = pl.BlockSpec((1, d), lambda i: (0, 0))
    return pl.pallas_call(
        _ln_kernel,
        grid=(s // tm,),
        in_specs=[row, vec, vec],
        out_specs=[row, row],
        out_shape=[jax.ShapeDtypeStruct((s, d), F32), jax.ShapeDtypeStruct((s, d), BF16)],
        compiler_params=_cparams(("parallel",)),
        name="ln_in",
    )(x2d, g.reshape(1, d), b.reshape(1, d))


def _rope_kernel(pos_ref, invf_ref, same_ref, up_ref, dn_ref):
    ang = pos_ref[...].astype(F32) * invf_ref[...]
    cos, sin = jnp.cos(ang), jnp.sin(ang)
    d = lax.broadcasted_iota(jnp.int32, ang.shape, 1) % ATT_QK_DIM
    same_ref[...] = jnp.where(d < ROT_DIM, cos, 1.0)
    up_ref[...] = jnp.where(d < ROT_HALF, -sin, 0.0)
    dn_ref[...] = jnp.where((d >= ROT_HALF) & (d < ROT_DIM), sin, 0.0)


def _rope_tables(pos_col, invf, tm):
    s = pos_col.shape[0]
    tab = pl.BlockSpec((tm, LANES), lambda i: (i, 0))
    return pl.pallas_call(
        _rope_kernel,
        grid=(s // tm,),
        in_specs=[pl.BlockSpec((tm, 1), lambda i: (i, 0)), pl.BlockSpec((1, LANES), lambda i: (0, 0))],
        out_specs=[tab, tab, tab],
        out_shape=[jax.ShapeDtypeStruct((s, LANES), F32)] * 3,
        compiler_params=_cparams(("parallel",)),
        name="rope_tables",
    )(pos_col, invf)


def _rotate(xg, same_ref, up_ref, dn_ref):
    up = pltpu.roll(xg, LANES - ROT_HALF, axis=1)
    dn = pltpu.roll(xg, ROT_HALF, axis=1)
    return xg * same_ref[...] + up * up_ref[...] + dn * dn_ref[...]


def _xw(x_ref, w_ref):
    return jnp.dot(x_ref[...], w_ref[...].astype(BF16), preferred_element_type=F32)


def _proj_q_kernel(x_ref, w_ref, same_ref, up_ref, dn_ref, o_ref, *, tq):
    acc = _xw(x_ref, w_ref)
    scale = LOG2E / math.sqrt(ATT_QK_DIM)
    for c in range(acc.shape[1] // LANES):
        rot = _rotate(acc[:, c * LANES:(c + 1) * LANES], same_ref, up_ref, dn_ref) * scale
        for b in range(o_ref.shape[0]):
            o_ref[b, c * LANES:(c + 1) * LANES, :] = rot[b * tq:(b + 1) * tq, :].T.astype(o_ref.dtype)


def _proj_k_kernel(x_ref, w_ref, same_ref, up_ref, dn_ref, o_ref):
    acc = _xw(x_ref, w_ref)
    for c in range(acc.shape[1] // LANES):
        cols = slice(c * LANES, (c + 1) * LANES)
        o_ref[:, cols] = _rotate(acc[:, cols], same_ref, up_ref, dn_ref).astype(o_ref.dtype)


def _proj_v_kernel(x_ref, w_ref, o_ref, *, tk):
    acc = _xw(x_ref, w_ref)
    ones = jnp.ones((BF16_ROWS, tk), o_ref.dtype)
    for b in range(o_ref.shape[0]):
        for hh in range(o_ref.shape[1]):
            blk = acc[b * tk:(b + 1) * tk, hh * HEAD_W:(hh + 1) * HEAD_W]
            o_ref[b, hh, 0:HEAD_W, :] = blk.T.astype(o_ref.dtype)
            o_ref[b, hh, HEAD_W:, :] = ones


def _proj_plain_kernel(x_ref, w_ref, o_ref):
    o_ref[...] = _xw(x_ref, w_ref).astype(o_ref.dtype)


def _proj_slab_kernel(x_ref, w_ref, o_ref):
    acc = _xw(x_ref, w_ref)
    for c in range(o_ref.shape[0]):
        o_ref[c] = acc[:, c * LANES:(c + 1) * LANES]


def _proj_gate_kernel(x_ref, w_ref, b_ref, o_ref):
    x = x_ref[...]
    for c in range(o_ref.shape[1] // GATE_CHUNK):
        cols = slice(c * GATE_CHUNK, (c + 1) * GATE_CHUNK)
        z = jnp.dot(x, w_ref[:, cols].astype(BF16), preferred_element_type=F32) + b_ref[:, cols]
        o_ref[:, cols] = (0.5 * jnp.tanh(0.5 * z) + 0.5).astype(o_ref.dtype)


def _in_projections(hb, w_in, b_gate, layer, rope, tm, tq):
    s, d = hb.shape
    tn = PROJ_TILE
    nb = s // tq
    x_spec = pl.BlockSpec((tm, d), lambda i, j: (i, 0))

    def w_spec(off, width=tn):
        return pl.BlockSpec((None, d, width), lambda i, j: (layer, 0, j + off // width))

    out_tile = pl.BlockSpec((tm, tn), lambda i, j: (i, j))
    rope_specs = [pl.BlockSpec((tm, LANES), lambda i, j: (i, 0))] * 3
    sem = _cparams(("parallel", "arbitrary"))

    qt = pl.pallas_call(
        functools.partial(_proj_q_kernel, tq=tq),
        grid=(s // tm, ATT_WIDTH // tn),
        in_specs=[x_spec, w_spec(0)] + rope_specs,
        out_specs=pl.BlockSpec((tm // tq, tn, tq), lambda i, j: (i, j, 0)),
        out_shape=jax.ShapeDtypeStruct((nb, ATT_WIDTH, tq), BF16),
        compiler_params=sem, name="proj_q",
    )(hb, w_in, *rope)

    k = pl.pallas_call(
        _proj_k_kernel,
        grid=(s // tm, ATT_WIDTH // tn),
        in_specs=[x_spec, w_spec(K_OFF)] + rope_specs,
        out_specs=out_tile,
        out_shape=jax.ShapeDtypeStruct((s, ATT_WIDTH), BF16),
        compiler_params=sem, name="proj_k",
    )(hb, w_in, *rope)

    vt = pl.pallas_call(
        functools.partial(_proj_v_kernel, tk=tq),
        grid=(s // tm, ATT_WIDTH // tn),
        in_specs=[x_spec, w_spec(V_OFF)],
        out_specs=pl.BlockSpec((tm // tq, tn // HEAD_W, VT_ROWS, tq), lambda i, j: (i, j, 0, 0)),
        out_shape=jax.ShapeDtypeStruct((nb, ATT_HEADS, VT_ROWS, tq), BF16),
        compiler_params=sem, name="proj_v",
    )(hb, w_in)

    u_pool = pl.pallas_call(
        _proj_plain_kernel,
        grid=(s // tm, POOL_WIDTH // tn),
        in_specs=[x_spec, w_spec(POOL_OFF)],
        out_specs=out_tile,
        out_shape=jax.ShapeDtypeStruct((s, POOL_WIDTH), F32),
        compiler_params=sem, name="proj_pool_in",
    )(hb, w_in)

    u_ssm = pl.pallas_call(
        _proj_slab_kernel,
        grid=(s // tm, SSM_WIDTH // tn),
        in_specs=[x_spec, w_spec(SSM_OFF)],
        out_specs=pl.BlockSpec((SSM_SLABS, tm, LANES), lambda i, j: (0, i, 0)),
        out_shape=jax.ShapeDtypeStruct((SSM_SLABS, s, LANES), F32),
        compiler_params=sem, name="proj_ssm_in",
    )(hb, w_in)

    gates = pl.pallas_call(
        _proj_gate_kernel,
        grid=(s // tm, GATE_COLS // GATE_TILE),
        in_specs=[x_spec, w_spec(GATE_OFF, GATE_TILE),
                  pl.BlockSpec((None, 1, GATE_TILE), lambda i, j: (layer, 0, j))],
        out_specs=pl.BlockSpec((tm, GATE_TILE), lambda i, j: (i, j)),
        out_shape=jax.ShapeDtypeStruct((s, GATE_COLS), BF16),
        compiler_params=sem, name="proj_gates",
    )(hb, w_in, b_gate.reshape(b_gate.shape[0], 1, GATE_COLS))
    return qt, k, vt, u_pool, u_ssm, gates


def _attn_kernel(lam_ref, qt_ref, k_ref, vt_ref, g_ref, o_ref,
                 sa1, sa2, sb1, sb2, ba1, ba2, bb1, bb2, m1_ref, a1_ref, m2_ref, a2_ref, *, tq):
    i = pl.program_id(1)
    qt = qt_ref[0]
    comp = lax.broadcasted_iota(jnp.int32, qt.shape, 0) < ATT_QK_DIM
    zero = jnp.zeros_like(qt)
    q1t = jnp.where(comp, qt, zero)
    q2t = jnp.where(comp, zero, qt)

    for m_ref, a_ref in ((m1_ref, a1_ref), (m2_ref, a2_ref)):
        m_ref[...] = jnp.full(m_ref.shape, NEG_BIG, F32)
        a_ref[...] = jnp.zeros(a_ref.shape, F32)

    buf_a = ((sa1, ba1), (sa2, ba2))
    buf_b = ((sb1, bb1), (sb2, bb2))

    half = tq // 2

    def scores(k0, rows, c0, buf, masked):
        kb = k_ref[pl.ds(pl.multiple_of(k0, half), rows), :]
        for qct, (s_ref, bm_ref) in zip((q1t, q2t), buf):
            s = jnp.dot(kb, qct[:, c0:], preferred_element_type=F32)
            if masked:
                keep = (lax.broadcasted_iota(jnp.int32, s.shape, 0)
                        <= lax.broadcasted_iota(jnp.int32, s.shape, 1))
                s = jnp.where(keep, s, NEG_BIG)
            s_ref[0:rows, c0:] = s
            bm_ref[:, c0:] = jnp.max(s, axis=0, keepdims=True)

    def consume(vb, rows, c0, buf):
        for (s_ref, bm_ref), m_ref, a_ref in zip(buf, (m1_ref, m2_ref), (a1_ref, a2_ref)):
            m_old = m_ref[:, c0:]
            m_new = jnp.maximum(m_old, bm_ref[:, c0:])
            alpha = jnp.exp2(m_old - m_new)
            p = jnp.exp2(s_ref[0:rows, c0:] - m_new).astype(BF16)
            a_ref[:, c0:] = alpha * a_ref[:, c0:] + jnp.dot(vb, p, preferred_element_type=F32)
            m_ref[:, c0:] = m_new

    def scores_full(j, buf):
        scores(j * tq, tq, 0, buf, False)

    def consume_full(j, buf):
        consume(vt_ref[j, 0], tq, 0, buf)

    def scores_diag_lo(buf):
        scores(i * tq, half, 0, buf, True)

    def consume_diag_lo(buf):
        consume(vt_ref[i, 0, :, 0:half], half, 0, buf)

    def scores_diag_hi(buf):
        scores(i * tq + half, half, half, buf, True)

    def consume_diag_hi(buf):
        consume(vt_ref[i, 0, :, half:], half, half, buf)

    scores_diag_lo(buf_a)

    @pl.when(i == 0)
    def _():
        scores_diag_hi(buf_b)
        consume_diag_lo(buf_a)
        consume_diag_hi(buf_b)

    @pl.when(i > 0)
    def _():
        scores_full(0, buf_b)
        consume_diag_lo(buf_a)
        npairs = lax.shift_right_logical(i - 1, 1)

        def pair(t, carry):
            scores_full(2 * t + 1, buf_a)
            consume_full(2 * t, buf_b)
            scores_full(2 * t + 2, buf_b)
            consume_full(2 * t + 1, buf_a)
            return carry

        lax.fori_loop(0, npairs, pair, 0)

        @pl.when(i % 2 == 1)
        def _():
            scores_diag_hi(buf_a)
            consume_full(i - 1, buf_b)
            consume_diag_hi(buf_a)

        @pl.when(i % 2 == 0)
        def _():
            scores_full(i - 1, buf_a)
            consume_full(i - 2, buf_b)
            scores_diag_hi(buf_b)
            consume_full(i - 1, buf_a)
            consume_diag_hi(buf_b)

    lam = lam_ref[0, 0]
    o = (a1_ref[0:HEAD_W, :] / a1_ref[HEAD_W:HEAD_W + 1, :]
         - lam * (a2_ref[0:HEAD_W, :] / a2_ref[HEAD_W:HEAD_W + 1, :]))
    o = o * lax.rsqrt(jnp.mean(o * o, axis=0, keepdims=True) + LN_EPS) * g_ref[...]
    o_ref[...] = o.T.astype(o_ref.dtype)


def _attention(qt, k, vt, lam, g_col, tq):
    s = k.shape[0]
    nb = s // tq
    score = pltpu.VMEM((tq, tq), F32)
    stat = pltpu.VMEM((1, tq), F32)
    accum = pltpu.VMEM((VT_ROWS, tq), F32)
    return pl.pallas_call(
        functools.partial(_attn_kernel, tq=tq),
        grid=(ATT_HEADS, nb),
        in_specs=[
            pl.BlockSpec(memory_space=pltpu.SMEM),
            pl.BlockSpec((1, HEAD_W, tq), lambda h, i: (i, h, 0)),
            pl.BlockSpec((s, HEAD_W), lambda h, i: (0, h)),
            pl.BlockSpec((nb, 1, VT_ROWS, tq), lambda h, i: (0, h, 0, 0)),
            pl.BlockSpec((HEAD_W, 1), lambda h, i: (0, 0)),
        ],
        out_specs=pl.BlockSpec((tq, HEAD_W), lambda h, i: (i, h)),
        out_shape=jax.ShapeDtypeStruct((s, ATT_WIDTH), BF16),
        scratch_shapes=[score, score, score, score, stat, stat, stat, stat, stat, accum, stat, accum],
        compiler_params=_cparams(("parallel", "arbitrary")),
        name="diff_attention",
    )(lam, qt, k, vt, g_col)


def _pool_kernel(u_ref, halo_ref, w_ref, sc_ref, o_ref, buf_ref, *, tm):
    i = pl.program_id(0)
    halo = halo_ref[...]
    buf_ref[0:POOL_HALO, :] = jnp.where(i > 0, halo, jnp.zeros_like(halo))
    buf_ref[POOL_HALO:, :] = u_ref[...]
    t = i * tm + lax.broadcasted_iota(jnp.int32, (tm, 1), 0)
    for g, w in enumerate(POOL_WINDOWS):
        cols = slice(g * POOL_CH, (g + 1) * POOL_CH)
        acc = buf_ref[POOL_HALO:, cols]
        for back in range(1, w):
            acc = acc + buf_ref[POOL_HALO - back:POOL_HALO - back + tm, cols]
        cnt = jnp.minimum(t + 1, w).astype(F32)
        pooled = acc / cnt - buf_ref[POOL_HALO:, cols]
        mixed = jnp.dot(pooled.astype(BF16), w_ref[g].astype(BF16), preferred_element_type=F32)
        o_ref[:, cols] = (mixed * sc_ref[:, cols]).astype(o_ref.dtype)


def _pool(u_pool, pool_w, pool_scale, layer, tm):
    s = u_pool.shape[0]
    return pl.pallas_call(
        functools.partial(_pool_kernel, tm=tm),
        grid=(s // tm,),
        in_specs=[
            pl.BlockSpec((tm, POOL_WIDTH), lambda i: (i, 0)),
            pl.BlockSpec((POOL_HALO, POOL_WIDTH),
                         lambda i: (jnp.maximum(i * (tm // POOL_HALO) - 1, 0), 0)),
            pl.BlockSpec((None, len(POOL_WINDOWS), POOL_CH, POOL_CH), lambda i: (layer, 0, 0, 0)),
            pl.BlockSpec((None, 1, POOL_WIDTH), lambda i: (layer, 0, 0)),
        ],
        out_specs=pl.BlockSpec((tm, POOL_WIDTH), lambda i: (i, 0)),
        out_shape=jax.ShapeDtypeStruct((s, POOL_WIDTH), BF16),
        scratch_shapes=[pltpu.VMEM((tm + POOL_HALO, POOL_WIDTH), F32)],
        compiler_params=_cparams(("parallel",)),
        name="pool",
    )(u_pool, u_pool, pool_w, pool_scale.reshape(pool_scale.shape[0], 1, POOL_WIDTH))


def _ssm_params(a_re, a_im, log_dt, b_re, b_im, c_re, c_im, d_skip):
    depth = a_re.shape[0]
    f = lambda x: x.astype(F32)
    ldt = jnp.broadcast_to(f(log_dt)[:, :, None], a_re.shape)
    row = lambda x: f(x).reshape(depth, SSM_SLABS, 1, SLAB_STATE)
    col = lambda x: f(x).reshape(depth, SSM_SLABS, SLAB_STATE, 1)
    bt = lambda x: (f(x).transpose(0, 3, 1, 2)
                    .reshape(depth, SSM_GROUP_CH, SSM_SLABS, SLAB_STATE).transpose(0, 2, 1, 3))
    cp = lambda x: (f(x).transpose(0, 3, 1, 2)
                    .reshape(depth, SSM_STATE, SSM_SLABS, LANES).transpose(0, 2, 1, 3))
    return (row(a_re), row(a_im), row(ldt), col(a_re), col(a_im), col(ldt),
            bt(b_re), bt(b_im), cp(c_re), cp(c_im), f(d_skip).reshape(depth, SSM_SLABS, 1, LANES))


def _gelu_tanh(x):
    return x * (0.5 * (1.0 + jnp.tanh(math.sqrt(2.0 / math.pi) * (x + 0.044715 * (x * x * x)))))


def _ssm_build(ar_ref, ai_ref, ldt_ref, arc_ref, aic_ref, ldtc_ref, btr_ref, bti_ref, cpr_ref, cpi_ref,
               m_ref, p_ref, q_ref, at_ref):
    t_len = SSM_CHUNK
    ar, ai = ar_ref[...], ai_ref[...]
    dt = jnp.exp(ldt_ref[...])
    mag = jnp.exp(ar * dt)
    ab_re, ab_im = mag * jnp.cos(ai * dt), mag * jnp.sin(ai * dt)
    den = ar * ar + ai * ai
    nr, ni = ab_re - 1.0, ab_im
    f_re = (nr * ar + ni * ai) / den
    f_im = (ni * ar - nr * ai) / den
    btr, bti = btr_ref[...], bti_ref[...]
    bb_re = f_re * btr - f_im * bti
    bb_im = f_re * bti + f_im * btr

    def tile_rows(x):
        return jnp.concatenate([x] * SLAB_GROUPS, axis=0)

    own_b = ((lax.broadcasted_iota(jnp.int32, (LANES, SLAB_STATE), 0) // SSM_GROUP_CH)
             == (lax.broadcasted_iota(jnp.int32, (LANES, SLAB_STATE), 1) // SSM_STATE))
    bbd_re = jnp.where(own_b, tile_rows(bb_re), 0.0)
    bbd_im = jnp.where(own_b, tile_rows(bb_im), 0.0)
    own_c = ((lax.broadcasted_iota(jnp.int32, (SLAB_STATE, LANES), 0) // SSM_STATE)
             == (lax.broadcasted_iota(jnp.int32, (SLAB_STATE, LANES), 1) // SSM_GROUP_CH))
    cbd_re = jnp.where(own_c, tile_rows(cpr_ref[...]), 0.0)
    cbd_im = jnp.where(own_c, tile_rows(cpi_ref[...]), 0.0)

    zero_blk = jnp.zeros((LANES, LANES), m_ref.dtype)
    for s in range(t_len):
        for t in range(s):
            m_ref[s * LANES:(s + 1) * LANES, t * LANES:(t + 1) * LANES] = zero_blk

    for k in range(t_len + 1):
        pm = jnp.exp(ar * dt * float(k))
        pr, pi = pm * jnp.cos(ai * dt * float(k)), pm * jnp.sin(ai * dt * float(k))
        if k == t_len:
            at_ref[0:1, :] = pr
            at_ref[1:2, :] = pi
            break
        wb_re = pr * bbd_re - pi * bbd_im
        wb_im = pr * bbd_im + pi * bbd_re
        lag_blk = (jnp.dot(wb_re, cbd_re, precision=HIGHEST, preferred_element_type=F32)
                   - jnp.dot(wb_im, cbd_im, precision=HIGHEST, preferred_element_type=F32))
        lag_blk = lag_blk.astype(m_ref.dtype)
        for s in range(t_len - k):
            m_ref[s * LANES:(s + 1) * LANES, (s + k) * LANES:(s + k + 1) * LANES] = lag_blk
        rows = slice((t_len - 1 - k) * LANES, (t_len - k) * LANES)
        p_ref[rows, 0:SLAB_STATE] = wb_re.astype(p_ref.dtype)
        p_ref[rows, SLAB_STATE:] = wb_im.astype(p_ref.dtype)

    arc, aic = arc_ref[...], aic_ref[...]
    dtc = jnp.exp(ldtc_ref[...])
    kk = lax.broadcasted_iota(jnp.int32, (SLAB_STATE, LANES), 1).astype(F32)
    pmc = jnp.exp(arc * dtc * kk)
    prc, pic = pmc * jnp.cos(aic * dtc * kk), pmc * jnp.sin(aic * dtc * kk)
    for t in range(t_len):
        pr_col, pi_col = prc[:, t + 1:t + 2], pic[:, t + 1:t + 2]
        cols = slice(t * LANES, (t + 1) * LANES)
        q_ref[0:SLAB_STATE, cols] = (pr_col * cbd_re - pi_col * cbd_im).astype(q_ref.dtype)
        q_ref[SLAB_STATE:, cols] = (-pi_col * cbd_re - pr_col * cbd_im).astype(q_ref.dtype)


def _ssm_kernel(u_ref, ar_ref, ai_ref, ldt_ref, arc_ref, aic_ref, ldtc_ref, btr_ref, bti_ref,
                cpr_ref, cpi_ref, d_ref, o_ref,
                m_ref, p_ref, q_ref, at_ref, urow_ref, xloc_ref, xprev_ref, st_ref, *, cm):
    t_len = SSM_CHUNK

    @pl.when(pl.program_id(1) == 0)
    def _():
        st_ref[...] = jnp.zeros(st_ref.shape, F32)
        _ssm_build(ar_ref, ai_ref, ldt_ref, arc_ref, aic_ref, ldtc_ref, btr_ref, bti_ref,
                   cpr_ref, cpi_ref, m_ref, p_ref, q_ref, at_ref)

    for t in range(t_len):
        urow_ref[:, t * LANES:(t + 1) * LANES] = u_ref[pl.ds(t, cm, stride=t_len), :]
    u = urow_ref[...]
    ub = u.astype(BF16)
    xloc_ref[...] = jnp.dot(ub, p_ref[...], preferred_element_type=F32)
    a_r = at_ref[0:1, :]
    a_i = at_ref[1:2, :]

    def step(c, carry):
        xr, xi = carry
        xprev_ref[pl.ds(c, 1), 0:SLAB_STATE] = xr
        xprev_ref[pl.ds(c, 1), SLAB_STATE:] = xi
        br = xloc_ref[pl.ds(c, 1), 0:SLAB_STATE]
        bi = xloc_ref[pl.ds(c, 1), SLAB_STATE:]
        return a_r * xr - a_i * xi + br, a_r * xi + a_i * xr + bi

    xr, xi = lax.fori_loop(0, cm, step, (st_ref[0:1, :], st_ref[1:2, :]))
    st_ref[0:1, :] = xr
    st_ref[1:2, :] = xi

    y = jnp.dot(ub, m_ref[...], preferred_element_type=F32)
    y = y + jnp.dot(xprev_ref[...].astype(BF16), q_ref[...], preferred_element_type=F32)
    d_row = jnp.concatenate([d_ref[...]] * t_len, axis=1)
    y = _gelu_tanh(y + d_row * u)
    for t in range(t_len):
        o_ref[pl.ds(t, cm, stride=t_len), :] = y[:, t * LANES:(t + 1) * LANES]


def _ssm(u_slabs, params, layer, cm):
    nslab, s, _ = u_slabs.shape
    rows = pl.BlockSpec((None, cm * SSM_CHUNK, LANES), lambda j, c: (j, c, 0))

    def per_slab(shape):
        return pl.BlockSpec((None, None) + shape, lambda j, c: (layer, j, 0, 0))

    lane_vec = per_slab((1, SLAB_STATE))
    col_vec = per_slab((SLAB_STATE, 1))
    bt_spec = per_slab((SSM_GROUP_CH, SLAB_STATE))
    cp_spec = per_slab((SSM_STATE, LANES))
    return pl.pallas_call(
        functools.partial(_ssm_kernel, cm=cm),
        grid=(nslab, s // (cm * SSM_CHUNK)),
        in_specs=[rows, lane_vec, lane_vec, lane_vec, col_vec, col_vec, col_vec,
                  bt_spec, bt_spec, cp_spec, cp_spec, per_slab((1, LANES))],
        out_specs=rows,
        out_shape=jax.ShapeDtypeStruct((nslab, s, LANES), F32),
        scratch_shapes=[pltpu.VMEM((SSM_ROW, SSM_ROW), BF16), pltpu.VMEM((SSM_ROW, 2 * SLAB_STATE), BF16),
                        pltpu.VMEM((2 * SLAB_STATE, SSM_ROW), BF16), pltpu.VMEM((2, SLAB_STATE), F32),
                        pltpu.VMEM((cm, SSM_ROW), F32),
                        pltpu.VMEM((cm, 2 * SLAB_STATE), F32), pltpu.VMEM((cm, 2 * SLAB_STATE), F32),
                        pltpu.VMEM((2, SLAB_STATE), F32)],
        compiler_params=_cparams(("parallel", "arbitrary")),
        name="s5_ssm",
    )(u_slabs, *params)


def _merge_kernel(ya_ref, yp_ref, ys_ref, g_ref, h_ref, wa_ref, wp_ref, ws_ref, wg_ref, bg_ref,
                  wo_ref, lg_ref, lb_ref, o_ref, ob_ref, *, alpha):
    ys = jnp.concatenate([ys_ref[c] for c in range(SSM_SLABS)], axis=1)
    z = jnp.dot(ys.astype(BF16), wg_ref[...], preferred_element_type=F32) + bg_ref[...]
    y_ssm = ys * jax.nn.sigmoid(z)
    merged = g_ref[:, 0:D_MODEL] * jnp.dot(ya_ref[...], wa_ref[...], preferred_element_type=F32)
    merged = merged + g_ref[:, D_MODEL:2 * D_MODEL] * jnp.dot(
        yp_ref[...], wp_ref[...], preferred_element_type=F32)
    merged = merged + g_ref[:, 2 * D_MODEL:] * jnp.dot(
        y_ssm.astype(BF16), ws_ref[...], preferred_element_type=F32)
    z = alpha * h_ref[...] + jnp.dot(merged.astype(BF16), wo_ref[...], preferred_element_type=F32)
    y = _ln_rows(z, lg_ref[...], lb_ref[...])
    o_ref[...] = y
    ob_ref[...] = y.astype(BF16)


def _merge(y_attn, y_pool, y_s, gates, h, wa, wp, ws, wg, bg, wo, ln_g, ln_b, layer, alpha, tm):
    s, d = h.shape

    def rows(width):
        return pl.BlockSpec((tm, width), lambda i: (i, 0))

    def whole(rows_, cols_):
        return pl.BlockSpec((None, rows_, cols_), lambda i: (layer, 0, 0), pipeline_mode=pl.Buffered(1))

    vec = lambda x: x.reshape(x.shape[0], 1, x.shape[1])
    return pl.pallas_call(
        functools.partial(_merge_kernel, alpha=alpha),
        grid=(s // tm,),
        in_specs=[rows(ATT_WIDTH), rows(POOL_WIDTH),
                  pl.BlockSpec((SSM_SLABS, tm, LANES), lambda i: (0, i, 0)),
                  rows(GATE_COLS), rows(d),
                  whole(ATT_WIDTH, d), whole(POOL_WIDTH, d), whole(SSM_WIDTH, d),
                  whole(SSM_WIDTH, SSM_WIDTH), whole(1, SSM_WIDTH), whole(d, d),
                  whole(1, d), whole(1, d)],
        out_specs=[rows(d), rows(d)],
        out_shape=[jax.ShapeDtypeStruct((s, d), F32), jax.ShapeDtypeStruct((s, d), BF16)],
        compiler_params=_cparams(("parallel",)),
        name="merge_out_ln",
    )(y_attn, y_pool, y_s, gates, h, wa, wp, ws, wg, vec(bg), wo, vec(ln_g), vec(ln_b))


def _mlp_kernel(hb_ref, h_ref, wu_ref, wd_ref, lg_ref, lb_ref, o_ref, ob_ref, *, alpha):
    f = pl.program_id(1)

    @pl.when(f == 0)
    def _():
        o_ref[...] = alpha * h_ref[...]

    up = jnp.dot(hb_ref[...], wu_ref[...], preferred_element_type=F32)
    r = jnp.maximum(up, 0.0)
    o_ref[...] += jnp.dot((r * r).astype(BF16), wd_ref[...], preferred_element_type=F32)

    @pl.when(f == pl.num_programs(1) - 1)
    def _():
        y = _ln_rows(o_ref[...], lg_ref[...], lb_ref[...])
        o_ref[...] = y
        ob_ref[...] = y.astype(BF16)


def _mlp(hb, h, w_up, w_down, ln_g, ln_b, layer, alpha, tm, tf):
    s, d = h.shape
    rows = pl.BlockSpec((tm, d), lambda i, f: (i, 0))
    vec = pl.BlockSpec((None, 1, d), lambda i, f: (layer, 0, 0))
    vec3 = lambda x: x.reshape(x.shape[0], 1, x.shape[1])
    return pl.pallas_call(
        functools.partial(_mlp_kernel, alpha=alpha),
        grid=(s // tm, D_FF // tf),
        in_specs=[rows,
                  pl.BlockSpec((tm, d), lambda i, f: (i, 0), pipeline_mode=pl.Buffered(1)),
                  pl.BlockSpec((None, d, tf), lambda i, f: (layer, 0, f)),
                  pl.BlockSpec((None, tf, d), lambda i, f: (layer, f, 0)),
                  vec, vec],
        out_specs=[rows, rows],
        out_shape=[jax.ShapeDtypeStruct((s, d), F32), jax.ShapeDtypeStruct((s, d), BF16)],
        compiler_params=_cparams(("parallel", "arbitrary")),
        name="mlp_ln",
    )(hb, h, w_up, w_down, vec3(ln_g), vec3(ln_b))


def kernel(x, positions, ln_in_g, ln_in_b, w_in, b_gate, lam_q1, lam_k1, lam_q2, lam_k2, subln_g, pool_w, pool_scale, ssm_a_re, ssm_a_im, ssm_log_dt, ssm_b_re, ssm_b_im, ssm_c_re, ssm_c_im, ssm_d, glu_w, glu_b, proj_attn, proj_pool, proj_ssm, w_out, ln1_g, ln1_b, w_up, w_down, ln2_g, ln2_b):
    bsz, seq, d = x.shape
    assert bsz == 1 and d == D_MODEL
    depth = w_in.shape[0]
    alpha = (2.0 * depth) ** 0.25
    tl = _tiles(seq)

    h, hb = _layer_norm(x.reshape(seq, d), ln_in_g, ln_in_b, tl["ln"])
    pos_col = positions.reshape(seq, 1)
    inv_freq = ROPE_THETA ** (-jnp.arange(0, ROT_DIM, 2, dtype=F32) / ROT_DIM)
    invf = jnp.tile(inv_freq, LANES // ROT_HALF).reshape(1, LANES)
    rope = _rope_tables(pos_col, invf, tl["proj"])
    ssm_params = _ssm_params(ssm_a_re, ssm_a_im, ssm_log_dt, ssm_b_re, ssm_b_im, ssm_c_re, ssm_c_im, ssm_d)
    merge_w = [w.astype(BF16) for w in (proj_attn, proj_pool, proj_ssm, glu_w)]
    w_out_b = w_out.astype(BF16)
    w_up_b, w_down_b = w_up.astype(BF16), w_down.astype(BF16)

    for l in range(depth):
        lam_init = 0.8 - 0.6 * math.exp(-0.3 * l)
        lam = (jnp.exp(jnp.sum(lam_q1[l].astype(F32) * lam_k1[l].astype(F32)))
               - jnp.exp(jnp.sum(lam_q2[l].astype(F32) * lam_k2[l].astype(F32))) + lam_init)
        g_col = (subln_g[l].astype(F32) * (1.0 - lam_init)).reshape(HEAD_W, 1)

        qt, k, vt, u_pool, u_ssm, gates = _in_projections(hb, w_in, b_gate, l, rope, tl["proj"], tl["att"])
        y_attn = _attention(qt, k, vt, lam.reshape(1, 1), g_col, tl["att"])
        y_pool = _pool(u_pool, pool_w, pool_scale, l, tl["pool"])
        y_s = _ssm(u_ssm, ssm_params, l, tl["ssm"])
        h, hb = _merge(y_attn, y_pool, y_s, gates, h, *merge_w, glu_b, w_out_b, ln1_g, ln1_b,
                       l, alpha, tl["merge"])
        h, hb = _mlp(hb, h, w_up_b, w_down_b, ln2_g, ln2_b, l, alpha, tl["mlp"], tl["ff"])
    return h.reshape(bsz, seq, d)
```
